```python
import functools
import jax, jax.numpy as jnp
from jax import lax
import numpy as np

D_MODEL = 2048
BATCH = 32
SEQ = 256
DEPTH = 2
DEC_BATCH = 8
DEC_SEQ = 1024
PAST_LEN = 512

GRID_W = 64
A_HEADS = 16
A_KV_HEADS = 4
A_GROUPS = A_HEADS // A_KV_HEADS
A_HEAD_DIM = 64
A_WIDTH = A_HEADS * A_HEAD_DIM
A_KV_WIDTH = A_KV_HEADS * A_HEAD_DIM
ATTN_SCALE = A_HEAD_DIM ** -0.5
WINDOW = 128
ATTN_BLOCK = 128
ROPE_BASE = 10000.0
SGU_CHUNK = 128
SGU_GROUPS = 8
SGU_WIDTH = 1024
SGU_GROUP_DIM = SGU_WIDTH // SGU_GROUPS
C_HEADS = 16
C_HEAD_DIM = 64
C_WIDTH = C_HEADS * C_HEAD_DIM
W_LORA = 64
A_LORA = 64
G_LORA = 128
C_COLS = 3 * C_WIDTH + W_LORA + A_LORA + G_LORA
N_BRANCH = 3
IN_COLS = A_WIDTH + 2 * A_KV_WIDTH + 2 * SGU_WIDTH + C_COLS + N_BRANCH * D_MODEL
D_FF = 5632
N_EXPERTS = 8
TOP_K = 2
D_FF_EXPERT = 7168
MOE_BLOCK = 128
N_DENSE = (DEPTH + 1) // 2
N_MOE = DEPTH // 2
ALPHA = (2 * DEPTH) ** 0.25
BETA = (8 * DEPTH) ** -0.25
LN_EPS = 1e-6
GN_EPS = C_HEAD_DIM * 1e-5
NEG_INF = -1e30

kernel_name = "hybrid_dit_swa_sgu_rwkv7_step"


def _layer_norm(x, g=None, b=None, eps=LN_EPS):
    xf = x.astype(jnp.float32)
    mu = jnp.mean(xf, axis=-1, keepdims=True)
    var = jnp.mean(jnp.square(xf - mu), axis=-1, keepdims=True)
    y = (xf - mu) * lax.rsqrt(var + eps)
    if g is not None:
        y = y * g.astype(jnp.float32) + b.astype(jnp.float32)
    return y.astype(x.dtype)


def _modulate(x, shift, scale):
    return _layer_norm(x) * (1 + scale[:, None, :]) + shift[:, None, :]


def _axial_rope(x):
    B, T, H, D = x.shape
    rows = T // GRID_W
    t = jnp.arange(rows * GRID_W)
    half = D // 2
    freqs = ROPE_BASE ** (-jnp.arange(0, half, 2, dtype=jnp.float32) / half)

    def rot(xh, pos):
        ang = pos.astype(jnp.float32)[:, None] * freqs[None, :]
        cos = jnp.cos(ang)[None, :, None, :]
        sin = jnp.sin(ang)[None, :, None, :]
        x1, x2 = jnp.split(xh, 2, axis=-1)
        return jnp.concatenate([x1 * cos - x2 * sin, x1 * sin + x2 * cos], axis=-1)

    xf = x.astype(jnp.float32)
    out = jnp.concatenate([rot(xf[..., :half], t // GRID_W), rot(xf[..., half:], t % GRID_W)], axis=-1)
    return out.astype(x.dtype)


def _attn_context(q, k, v, sink):
    B, L, H, D = q.shape
    qg = q.reshape(B, L, A_KV_HEADS, A_GROUPS, D)
    s = jnp.einsum("bqkgd,bskd->bkgqs", qg, k, preferred_element_type=jnp.float32) * ATTN_SCALE
    sink_logit = jnp.broadcast_to(sink.astype(jnp.float32).reshape(1, A_KV_HEADS, A_GROUPS, 1, 1), s.shape[:-1] + (1,))
    p = jax.nn.softmax(jnp.concatenate([s, sink_logit], axis=-1), axis=-1)[..., :-1]
    o = jnp.einsum("bkgqs,bskd->bqkgd", p.astype(v.dtype), v)
    return o.reshape(B, L, H * D)


def _attn_latent(q, k, v, k_ctx, v_ctx, sink):
    B, T, H, D = q.shape
    nb = T // ATTN_BLOCK
    span = ATTN_BLOCK + 2 * WINDOW
    qb = q.reshape(B, nb, ATTN_BLOCK, A_KV_HEADS, A_GROUPS, D)
    idx = jnp.arange(nb)[:, None] * ATTN_BLOCK + jnp.arange(span)[None, :]
    pad = ((0, 0), (WINDOW, WINDOW), (0, 0), (0, 0))
    kb = jnp.pad(k, pad)[:, idx]
    vb = jnp.pad(v, pad)[:, idx]
    key_pos = idx - WINDOW
    q_pos = jnp.arange(nb)[:, None] * ATTN_BLOCK + jnp.arange(ATTN_BLOCK)[None, :]
    valid = ((key_pos[:, None, :] >= 0) & (key_pos[:, None, :] < T)
             & (jnp.abs(q_pos[:, :, None] - key_pos[:, None, :]) <= WINDOW))
    s_loc = jnp.einsum("bnqkgd,bnskd->bkgnqs", qb, kb, preferred_element_type=jnp.float32) * ATTN_SCALE
    s_loc = jnp.where(valid, s_loc, NEG_INF)
    s_ctx = jnp.einsum("bnqkgd,blkd->bkgnql", qb, k_ctx, preferred_element_type=jnp.float32) * ATTN_SCALE
    sink_logit = jnp.broadcast_to(sink.astype(jnp.float32).reshape(1, A_KV_HEADS, A_GROUPS, 1, 1, 1), s_loc.shape[:-1] + (1,))
    p = jax.nn.softmax(jnp.concatenate([s_loc, s_ctx, sink_logit], axis=-1), axis=-1).astype(v.dtype)
    o = (jnp.einsum("bkgnqs,bnskd->bnqkgd", p[..., :span], vb)
         + jnp.einsum("bkgnql,blkd->bnqkgd", p[..., span:-1], v_ctx))
    return o.reshape(B, T, H * D)


def _sgu_mixer(xb, norm_g, w_s, b_s):
    B, T, _ = xb.shape
    u, v = jnp.split(jax.nn.gelu(xb), 2, axis=-1)
    vg = v.reshape(B, T, SGU_GROUPS, SGU_GROUP_DIM).astype(jnp.float32)
    mu = jnp.mean(vg, axis=-1, keepdims=True)
    var = jnp.mean(jnp.square(vg - mu), axis=-1, keepdims=True)
    vn = (vg - mu) * lax.rsqrt(var + LN_EPS) * norm_g.astype(jnp.float32).reshape(SGU_GROUPS, SGU_GROUP_DIM)
    vc = vn.reshape(B, T // SGU_CHUNK, SGU_CHUNK, SGU_GROUPS, SGU_GROUP_DIM)
    mixed = (jnp.einsum("gpq,bnqgc->bnpgc", w_s.astype(jnp.float32), vc)
             + jnp.swapaxes(b_s.astype(jnp.float32), 0, 1)[None, None, :, :, None])
    return u * mixed.reshape(B, T, SGU_WIDTH).astype(u.dtype)


def _centred_shift(x, mu):
    prev = jnp.pad(x[:, :-1], ((0, 0), (1, 0), (0, 0)))
    nxt = jnp.pad(x[:, 1:], ((0, 0), (0, 1), (0, 0)))
    return x + mu[0] * (prev - x) + mu[1] * (nxt - x)


def _rwkv_scan(s0, r, w, k, v, kk, a, reverse):
    xs = tuple(jnp.moveaxis(t, 1, 0) for t in (r, w, k, v, kk, a))

    def step(s, inp):
        r_t, w_t, k_t, v_t, kk_t, a_t = inp
        sa = jnp.einsum("bhij,bhj->bhi", s, -kk_t)
        s = (s * w_t[:, :, None, :] + sa[..., None] * (kk_t * a_t)[:, :, None, :]
             + v_t[..., None] * k_t[:, :, None, :])
        return s, jnp.einsum("bhij,bhj->bhi", s, r_t)

    s_fin, ys = lax.scan(step, s0, xs, reverse=reverse)
    return s_fin, jnp.moveaxis(ys, 0, 1)


def _rwkv_mixer(xc, lp, s_f0, s_b0):
    B, T, _ = xc.shape
    f32 = jnp.float32
    xf = xc.astype(f32)
    c3 = 3 * C_WIDTH
    r, k, v, wl, al, gl = jnp.split(xf, [C_WIDTH, 2 * C_WIDTH, c3, c3 + W_LORA, c3 + W_LORA + A_LORA], axis=-1)

    def heads(t):
        return t.reshape(B, T, C_HEADS, C_HEAD_DIM)

    kk = heads(k * lp["rwkv_k_k"].astype(f32))
    kk = kk * lax.rsqrt(jnp.maximum(jnp.sum(kk * kk, axis=-1, keepdims=True), 1e-24))
    gate = jnp.einsum("btr,rc->btc", jax.nn.sigmoid(gl), lp["rwkv_g2"].astype(f32))
    tw = jnp.tanh(wl)
    k_a = lp["rwkv_k_a"].astype(f32)
    r_k = lp["rwkv_r_k"].astype(f32)
    rh, vh = heads(r), heads(v)
    ys, bonuses, finals = [], [], []
    for d, s0 in enumerate((s_f0, s_b0)):
        w_log = -jax.nn.softplus(-(lp["rwkv_w0"][d].astype(f32) + jnp.einsum("btr,rc->btc", tw, lp["rwkv_w2"][d].astype(f32)))) - 0.5
        decay = jnp.exp(-jnp.exp(w_log))
        a_d = jax.nn.sigmoid(lp["rwkv_a0"][d].astype(f32) + jnp.einsum("btr,rc->btc", al, lp["rwkv_a2"][d].astype(f32)))
        k_d = heads(k * (1 + (a_d - 1) * k_a))
        s_fin, y_d = _rwkv_scan(s0, rh, heads(decay), k_d, vh, kk, heads(a_d), d == 1)
        ys.append(y_d)
        bonuses.append(jnp.sum(rh * k_d * r_k, axis=-1, keepdims=True) * vh)
        finals.append(s_fin)
    y = ys[0] + ys[1]
    mu = jnp.mean(y, axis=-1, keepdims=True)
    var = jnp.mean(jnp.square(y - mu), axis=-1, keepdims=True)
    y = ((y - mu) * lax.rsqrt(var + GN_EPS)).reshape(B, T, C_WIDTH)
    y = y * lp["rwkv_ln_g"].astype(f32) + lp["rwkv_ln_b"].astype(f32)
    out = (y + (bonuses[0] + bonuses[1]).reshape(B, T, C_WIDTH)) * gate
    return out.astype(xc.dtype), finals[0], finals[1]


def _mixers(h, lp, ctx):
    B, T, _ = h.shape
    proj = jnp.einsum("btd,dc->btc", h, lp["w_in"])
    o1 = A_WIDTH
    o2 = o1 + A_KV_WIDTH
    o3 = o2 + A_KV_WIDTH
    o4 = o3 + 2 * SGU_WIDTH
    o5 = o4 + C_COLS
    q, k, v, xb, xc, gl = jnp.split(proj, [o1, o2, o3, o4, o5], axis=-1)
    q = q.reshape(B, T, A_HEADS, A_HEAD_DIM)
    k = k.reshape(B, T, A_KV_HEADS, A_HEAD_DIM)
    v = v.reshape(B, T, A_KV_HEADS, A_HEAD_DIM)
    if ctx is None:
        attn = _attn_context(q, k, v, lp["attn_sink"])
        s_f0 = jnp.zeros((B, C_HEADS, C_HEAD_DIM, C_HEAD_DIM), jnp.float32)
        s_b0 = s_f0
    else:
        k_ctx, v_ctx, s_f0, s_b0 = ctx
        attn = _attn_latent(_axial_rope(q), _axial_rope(k), v, k_ctx, v_ctx, lp["attn_sink"])
    sgu = _sgu_mixer(xb, lp["sgu_norm_g"], lp["sgu_w"], lp["sgu_b"])
    rw, s_f, s_b = _rwkv_mixer(_centred_shift(xc, lp["rwkv_mu"]), lp, s_f0, s_b0)
    g_a, g_b, g_c = jnp.split(jax.nn.sigmoid(gl + lp["b_gate"]), N_BRANCH, axis=-1)
    merged = (g_a * jnp.einsum("btc,cd->btd", attn, lp["w_branch_a"])
              + g_b * jnp.einsum("btc,cd->btd", sgu, lp["w_branch_b"])
              + g_c * jnp.einsum("btc,cd->btd", rw, lp["w_branch_c"]))
    out = jnp.einsum("btd,de->bte", merged, lp["w_out"])
    if ctx is None:
        return out, (k, v, s_f, s_b)
    return out, None


def _swiglu(x, w_in, w_out):
    g, u = jnp.split(jnp.einsum("btd,df->btf", x, w_in), 2, axis=-1)
    return jnp.einsum("btf,fd->btd", jax.nn.silu(g) * u, w_out)


def _moe_swiglu(x, router, w_in, w_out):
    B, T, D = x.shape
    n = B * T
    xf = x.reshape(n, D)
    logits = jnp.einsum("nd,de->ne", xf, router, preferred_element_type=jnp.float32)
    top_val, top_idx = lax.top_k(logits, TOP_K)
    gates = jax.nn.softmax(top_val, axis=-1)
    flat_e = top_idx.reshape(-1)
    flat_tok = jnp.repeat(jnp.arange(n, dtype=jnp.int32), TOP_K)
    order = jnp.argsort(flat_e)
    e_sorted = flat_e[order]
    tok_sorted = flat_tok[order]
    g_sorted = gates.reshape(-1)[order]
    counts = jnp.zeros((N_EXPERTS,), jnp.int32).at[flat_e].add(1)
    padded = (counts + MOE_BLOCK - 1) // MOE_BLOCK * MOE_BLOCK
    pad_end = jnp.cumsum(padded)
    pad_start = pad_end - padded
    start_sorted = jnp.cumsum(counts) - counts
    n_assign = n * TOP_K
    dest = pad_start[e_sorted] + jnp.arange(n_assign, dtype=jnp.int32) - start_sorted[e_sorted]
    n_rows = -(-n_assign // MOE_BLOCK) * MOE_BLOCK + N_EXPERTS * MOE_BLOCK
    n_blocks = n_rows // MOE_BLOCK
    row_tok = jnp.zeros((n_rows,), jnp.int32).at[dest].set(tok_sorted)
    block_e = jnp.minimum(jnp.searchsorted(pad_end, jnp.arange(n_blocks, dtype=jnp.int32) * MOE_BLOCK, side="right"), N_EXPERTS - 1)

    def expert_block(args):
        xb, e = args
        g, u = jnp.split(xb @ w_in[e], 2, axis=-1)
        return (jax.nn.silu(g) * u) @ w_out[e]

    ys = lax.map(expert_block, (xf[row_tok].reshape(n_blocks, MOE_BLOCK, D), block_e)).reshape(n_rows, D)
    y = jnp.zeros_like(xf).at[tok_sorted].add(ys[dest] * g_sorted[:, None].astype(ys.dtype))
    return y.reshape(B, T, D)


def _layer(x, cond, lp, ffn, ctx):
    mod = jnp.einsum("bd,de->be", jax.nn.silu(cond), lp["w_mod"]) + lp["b_mod"]
    sh1, sc1, g1, sh2, sc2, g2 = jnp.split(mod, 6, axis=-1)
    mix, cache = _mixers(_modulate(x, sh1, sc1), lp, ctx)
    x = _layer_norm(ALPHA * x + g1[:, None, :] * mix, lp["ln1_g"], lp["ln1_b"])
    x = _layer_norm(ALPHA * x + g2[:, None, :] * ffn(_modulate(x, sh2, sc2)), lp["ln2_g"], lp["ln2_b"])
    return x, cache


def setup_inputs(seed: int = 0) -> dict:
    key = jax.random.key(seed)
    ks = iter(jax.random.split(key, 64))
    D = D_MODEL

    def nrm(shape, scale):
        return scale * jax.random.normal(next(ks), shape, jnp.float32)

    def unif(shape, lo, hi):
        return jax.random.uniform(next(ks), shape, jnp.float32, lo, hi)

    return {
        "x_prompt": nrm((BATCH, SEQ, D), 1.0),
        "x_sample": nrm((DEC_BATCH, DEC_SEQ, D), 1.0),
        "cache_k": nrm((DEC_BATCH, DEPTH, PAST_LEN, A_KV_HEADS, A_HEAD_DIM), 1.0),
        "cache_v": nrm((DEC_BATCH, DEPTH, PAST_LEN, A_KV_HEADS, A_HEAD_DIM), 1.0),
        "state_rwkv": nrm((DEC_BATCH, DEPTH, 2, C_HEADS, C_HEAD_DIM, C_HEAD_DIM), 1.0),
        "c": nrm((DEC_BATCH, D), 1.0),
        "c_ctx": nrm((D,), 1.0),
        "w_mod": nrm((DEPTH, D, 6 * D), 0.5 * D ** -0.5),
        "b_mod": nrm((DEPTH, 6 * D), 0.02),
        "w_in": nrm((DEPTH, D, IN_COLS), D ** -0.5),
        "b_gate": nrm((DEPTH, N_BRANCH * D), 0.02),
        "attn_sink": nrm((DEPTH, A_HEADS), 1.0),
        "sgu_norm_g": 1.0 + nrm((DEPTH, SGU_WIDTH), 0.05),
        "sgu_w": nrm((DEPTH, SGU_GROUPS, SGU_CHUNK, SGU_CHUNK), SGU_CHUNK ** -0.5),
        "sgu_b": 1.0 + nrm((DEPTH, SGU_GROUPS, SGU_CHUNK), 0.1),
        "rwkv_mu": unif((DEPTH, 2, C_COLS), 0.0, 0.5),
        "rwkv_w0": unif((DEPTH, 2, C_WIDTH), -4.0, 0.0),
        "rwkv_w2": nrm((DEPTH, 2, W_LORA, C_WIDTH), 0.5 * W_LORA ** -0.5),
        "rwkv_a0": nrm((DEPTH, 2, C_WIDTH), 0.5),
        "rwkv_a2": nrm((DEPTH, 2, A_LORA, C_WIDTH), 0.5 * A_LORA ** -0.5),
        "rwkv_g2": nrm((DEPTH, G_LORA, C_WIDTH), G_LORA ** -0.5),
        "rwkv_k_k": 0.85 + nrm((DEPTH, C_WIDTH), 0.05),
        "rwkv_k_a": 1.0 + nrm((DEPTH, C_WIDTH), 0.05),
        "rwkv_r_k": nrm((DEPTH, C_HEADS, C_HEAD_DIM), 0.1),
        "rwkv_ln_g": 1.0 + nrm((DEPTH, C_WIDTH), 0.05),
        "rwkv_ln_b": nrm((DEPTH, C_WIDTH), 0.02),
        "w_branch_a": nrm((DEPTH, A_WIDTH, D), A_WIDTH ** -0.5),
        "w_branch_b": nrm((DEPTH, SGU_WIDTH, D), SGU_WIDTH ** -0.5),
        "w_branch_c": nrm((DEPTH, C_WIDTH, D), C_WIDTH ** -0.5),
        "w_out": nrm((DEPTH, D, D), BETA * D ** -0.5),
        "ln1_g": 1.0 + nrm((DEPTH, D), 0.05),
        "ln1_b": nrm((DEPTH, D), 0.02),
        "ln2_g": 1.0 + nrm((DEPTH, D), 0.05),
        "ln2_b": nrm((DEPTH, D), 0.02),
        "ffn_w_in": nrm((N_DENSE, D, 2 * D_FF), D ** -0.5),
        "ffn_w_out": nrm((N_DENSE, D_FF, D), BETA * D_FF ** -0.5),
        "moe_router": nrm((N_MOE, D, N_EXPERTS), D ** -0.5),
        "moe_w_in": nrm((N_MOE, N_EXPERTS, D, 2 * D_FF_EXPERT), D ** -0.5),
        "moe_w_out": nrm((N_MOE, N_EXPERTS, D_FF_EXPERT, D), BETA * D_FF_EXPERT ** -0.5),
    }


def reference(x_prompt, x_sample, cache_k, cache_v, state_rwkv, c, c_ctx, w_mod, b_mod, w_in, b_gate,
              attn_sink, sgu_norm_g, sgu_w, sgu_b, rwkv_mu, rwkv_w0, rwkv_w2, rwkv_a0, rwkv_a2, rwkv_g2,
              rwkv_k_k, rwkv_k_a, rwkv_r_k, rwkv_ln_g, rwkv_ln_b, w_branch_a, w_branch_b, w_branch_c, w_out,
              ln1_g, ln1_b, ln2_g, ln2_b, ffn_w_in, ffn_w_out, moe_router, moe_w_in, moe_w_out):
    def layer_params(l):
        return {
            "w_mod": w_mod[l], "b_mod": b_mod[l], "w_in": w_in[l], "b_gate": b_gate[l],
            "attn_sink": attn_sink[l], "sgu_norm_g": sgu_norm_g[l], "sgu_w": sgu_w[l], "sgu_b": sgu_b[l],
            "rwkv_mu": rwkv_mu[l], "rwkv_w0": rwkv_w0[l], "rwkv_w2": rwkv_w2[l], "rwkv_a0": rwkv_a0[l],
            "rwkv_a2": rwkv_a2[l], "rwkv_g2": rwkv_g2[l], "rwkv_k_k": rwkv_k_k[l], "rwkv_k_a": rwkv_k_a[l],
            "rwkv_r_k": rwkv_r_k[l], "rwkv_ln_g": rwkv_ln_g[l], "rwkv_ln_b": rwkv_ln_b[l],
            "w_branch_a": w_branch_a[l], "w_branch_b": w_branch_b[l], "w_branch_c": w_branch_c[l],
            "w_out": w_out[l], "ln1_g": ln1_g[l], "ln1_b": ln1_b[l], "ln2_g": ln2_g[l], "ln2_b": ln2_b[l],
        }

    def ffn_for(l):
        m = l // 2
        if l % 2 == 0:
            return functools.partial(_swiglu, w_in=ffn_w_in[m], w_out=ffn_w_out[m])
        return functools.partial(_moe_swiglu, router=moe_router[m], w_in=moe_w_in[m], w_out=moe_w_out[m])

    y_prompt = x_prompt
    ks, vs, sts = [], [], []
    for l in range(DEPTH):
        y_prompt, (k_l, v_l, sf_l, sb_l) = _layer(y_prompt, c_ctx[None, :], layer_params(l), ffn_for(l), None)
        ks.append(k_l)
        vs.append(v_l)
        sts.append(jnp.stack([sf_l, sb_l], axis=1))
    new_cache_k = jnp.stack(ks, axis=1)
    new_cache_v = jnp.stack(vs, axis=1)
    new_state_rwkv = jnp.stack(sts, axis=1)

    y_sample = x_sample
    for l in range(DEPTH):
        ctx = (cache_k[:, l], cache_v[:, l],
               state_rwkv[:, l, 0].astype(jnp.float32), state_rwkv[:, l, 1].astype(jnp.float32))
        y_sample, _ = _layer(y_sample, c, layer_params(l), ffn_for(l), ctx)

    return (y_prompt, y_sample, new_cache_k, new_cache_v, new_state_rwkv)
```

```python
import functools

import jax
import jax.numpy as jnp
from jax import lax
from jax.experimental import pallas as pl
from jax.experimental.pallas import tpu as pltpu

F32 = jnp.float32
BF16 = jnp.bfloat16
HIGHEST = lax.Precision.HIGHEST

D_MODEL = 2048
DEPTH = 2
GRID_W = 64
A_HEADS = 16
A_KV_HEADS = 4
A_GROUPS = A_HEADS // A_KV_HEADS
HEAD_DIM = 64
A_WIDTH = A_HEADS * HEAD_DIM
A_KV_WIDTH = A_KV_HEADS * HEAD_DIM
ATTN_SCALE = HEAD_DIM ** -0.5
WINDOW = 128
ATTN_BLOCK = 128
ROPE_BASE = 10000.0
SGU_CHUNK = 128
SGU_GROUPS = 8
SGU_WIDTH = 1024
C_HEADS = 16
C_WIDTH = C_HEADS * HEAD_DIM
W_LORA = 64
A_LORA = 64
G_LORA = 128
LORA_COLS = W_LORA + A_LORA + G_LORA
C_COLS = 3 * C_WIDTH + LORA_COLS
N_BRANCH = 3
D_FF = 5632
N_EXPERTS = 8
TOP_K = 2
D_FF_EXPERT = 7168
ALPHA = (2 * DEPTH) ** 0.25
LN_EPS = 1e-6
GN_EPS = HEAD_DIM * 1e-5
NEG_INF = -1e30

LANES = 128
HEAD_PAIRS = C_HEADS * HEAD_DIM // LANES
SCAN_CHUNK = 64
VMEM_LIMIT = 56 * 2 ** 20

COL_TILE = 1024
N_COL_TILES = 13
PROJ_COLS = N_COL_TILES * COL_TILE
CT_GATE, CT_SGU_U, CT_SGU_V, CT_Q, CT_RKV, CT_MISC = 0, 6, 7, 8, 9, 12

IN_TM = 1024
MERGE_TM = 256
FFN_TM = 512
FFN_TF = 512
PRE_TM = 256
POST_TM = 512
SGU_TM = 512
ROUTE_TM = 512
MOE_TM = 1024
MOE_SUB = 256
MOE_TF = 512
COMB_TM = 256


def _params(sem):
    return pltpu.CompilerParams(dimension_semantics=sem, vmem_limit_bytes=VMEM_LIMIT)


def _dot(a, b, precision=None):
    return jnp.dot(a, b, preferred_element_type=F32, precision=precision)


def _dot_nt(a, b):
    return lax.dot_general(a, b, (((1,), (1,)), ((), ())), preferred_element_type=F32)


def _dot_tn(a, b):
    return lax.dot_general(a, b, (((0,), (0,)), ((), ())), preferred_element_type=F32)


def _layer_norm(x):
    mu = jnp.mean(x, axis=-1, keepdims=True)
    xc = x - mu
    var = jnp.mean(xc * xc, axis=-1, keepdims=True)
    return xc * lax.rsqrt(var + LN_EPS)


def _gelu(x):
    return 0.5 * x * (1.0 + jnp.tanh(0.7978845608028654 * (x + 0.044715 * (x * x * x))))


def _sigmoid(x):
    return 1.0 / (1.0 + jnp.exp(-x))


def _request_of_tile(i, tm, m_prompt, t_sample):
    n_prompt_tiles = m_prompt // tm
    return jnp.where(i < n_prompt_tiles, 0, 1 + (i - n_prompt_tiles) // (t_sample // tm))


def _mod_kernel(c_ref, w_ref, b_ref, o_ref):
    c = c_ref[...]
    h = (c * _sigmoid(c)).astype(BF16)
    o_ref[0] = _dot(h, w_ref[0].astype(BF16)) + b_ref[0]


def _modulation(cond, w_mod, b_mod):
    tn = 1024
    n = w_mod.shape[-1]
    return pl.pallas_call(
        _mod_kernel,
        grid=(DEPTH, n // tn),
        in_specs=[
            pl.BlockSpec((16, D_MODEL), lambda l, j: (0, 0)),
            pl.BlockSpec((1, D_MODEL, tn), lambda l, j: (l, 0, j)),
            pl.BlockSpec((1, 1, tn), lambda l, j: (l, 0, j)),
        ],
        out_specs=pl.BlockSpec((1, 16, tn), lambda l, j: (l, 0, j)),
        out_shape=jax.ShapeDtypeStruct((DEPTH, 16, n), F32),
        compiler_params=_params(("parallel", "parallel")),
        name="modulation",
    )(cond, w_mod, b_mod.reshape(DEPTH, 1, n))


def _in_kernel(x_ref, sh_ref, sc_ref, w_ref, o_ref, h_scr):
    @pl.when(pl.program_id(1) == 0)
    def _():
        y = _layer_norm(x_ref[...])
        h_scr[...] = (y * (1.0 + sc_ref[0]) + sh_ref[0]).astype(BF16)

    o_ref[...] = _dot(h_scr[...], w_ref[...])


def _in_proj(x, mod3, w_in_p, m_prompt, t_sample):
    m = x.shape[0]
    req = functools.partial(_request_of_tile, tm=IN_TM, m_prompt=m_prompt, t_sample=t_sample)
    return pl.pallas_call(
        _in_kernel,
        grid=(m // IN_TM, N_COL_TILES),
        in_specs=[
            pl.BlockSpec((IN_TM, D_MODEL), lambda i, j: (i, 0)),
            pl.BlockSpec((1, 1, D_MODEL), lambda i, j: (req(i) * 6 + 0, 0, 0)),
            pl.BlockSpec((1, 1, D_MODEL), lambda i, j: (req(i) * 6 + 1, 0, 0)),
            pl.BlockSpec((D_MODEL, COL_TILE), lambda i, j: (0, j)),
        ],
        out_specs=pl.BlockSpec((IN_TM, COL_TILE), lambda i, j: (i, j)),
        out_shape=jax.ShapeDtypeStruct((m, PROJ_COLS), F32),
        scratch_shapes=[pltpu.VMEM((IN_TM, D_MODEL), BF16)],
        compiler_params=_params(("parallel", "arbitrary")),
        name="in_proj",
    )(x, mod3, mod3, w_in_p)


def _softmax_parts(scores, sink):
    m = sink
    for s in scores:
        m = jnp.maximum(m, jnp.max(s, axis=-1, keepdims=True))
    ps = [jnp.exp(s - m) for s in scores]
    den = jnp.exp(sink - m)
    for p in ps:
        den = den + jnp.sum(p, axis=-1, keepdims=True)
    inv = 1.0 / den
    return [(p * inv).astype(BF16) for p in ps]


def _attn_ctx_kernel(sink_ref, q_ref, kv_ref, o_ref):
    for kvh in range(A_KV_HEADS):
        k = kv_ref[:, kvh * HEAD_DIM:(kvh + 1) * HEAD_DIM].astype(BF16)
        v = kv_ref[:, A_KV_WIDTH + kvh * HEAD_DIM:A_KV_WIDTH + (kvh + 1) * HEAD_DIM].astype(BF16)
        for g in range(A_GROUPS):
            h = kvh * A_GROUPS + g
            q = q_ref[:, h * HEAD_DIM:(h + 1) * HEAD_DIM].astype(BF16)
            s = _dot_nt(q, k) * ATTN_SCALE
            (p,) = _softmax_parts([s], sink_ref[h])
            o_ref[:, h * HEAD_DIM:(h + 1) * HEAD_DIM] = _dot(p, v).astype(BF16)


def _attn_context(proj, sink, n_seq, t):
    kv_w = 2 * A_KV_WIDTH
    return pl.pallas_call(
        _attn_ctx_kernel,
        grid=(n_seq,),
        in_specs=[
            pl.BlockSpec(memory_space=pltpu.SMEM),
            pl.BlockSpec((t, A_WIDTH), lambda b: (b, CT_Q)),
            pl.BlockSpec((t, kv_w), lambda b: (b, CT_MISC * COL_TILE // kv_w)),
        ],
        out_specs=pl.BlockSpec((t, A_WIDTH), lambda b: (b, 0)),
        out_shape=jax.ShapeDtypeStruct((n_seq * t, A_WIDTH), BF16),
        compiler_params=_params(("parallel",)),
        name="attn_context",
    )(sink, proj, proj)


def _rope(x, cos, sin_signed):
    w = x.shape[-1]
    lane = lax.broadcasted_iota(jnp.int32, x.shape, 1)
    swapped = jnp.where((lane % 32) < 16, pltpu.roll(x, w - 16, 1), pltpu.roll(x, 16, 1))
    return x * cos + swapped * sin_signed


def _attn_lat_kernel(sink_ref, q_ref, kv_ref, cq_ref, sq_ref, ck_ref, sk_ref, kc_ref, vc_ref, o_ref,
                     k_scr, v_scr, kc_scr, vc_scr, *, t):
    n = pl.program_id(1)
    span = ATTN_BLOCK + 2 * WINDOW

    @pl.when(n == 0)
    def _():
        k_scr[...] = _rope(kv_ref[:, :A_KV_WIDTH], ck_ref[...], sk_ref[...]).astype(BF16)
        v_scr[...] = kv_ref[:, A_KV_WIDTH:].astype(BF16)
        kc_scr[...] = kc_ref[0, 0].astype(BF16)
        vc_scr[...] = vc_ref[0, 0].astype(BF16)

    q = _rope(q_ref[...], cq_ref[...], sq_ref[...])
    start = pl.multiple_of(jnp.clip(n * ATTN_BLOCK - WINDOW, 0, t - span), ATTN_BLOCK)
    k_loc = k_scr[pl.ds(start, span), :]
    v_loc = v_scr[pl.ds(start, span), :]
    q_pos = n * ATTN_BLOCK + lax.broadcasted_iota(jnp.int32, (ATTN_BLOCK, span), 0)
    k_pos = start + lax.broadcasted_iota(jnp.int32, (ATTN_BLOCK, span), 1)
    valid = jnp.logical_and(q_pos - k_pos <= WINDOW, k_pos - q_pos <= WINDOW)
    for kvh in range(A_KV_HEADS):
        hs = slice(kvh * HEAD_DIM, (kvh + 1) * HEAD_DIM)
        kl, vl, kc, vc = k_loc[:, hs], v_loc[:, hs], kc_scr[:, hs], vc_scr[:, hs]
        for g in range(A_GROUPS):
            h = kvh * A_GROUPS + g
            qh = q[:, h * HEAD_DIM:(h + 1) * HEAD_DIM].astype(BF16)
            s_loc = jnp.where(valid, _dot_nt(qh, kl) * ATTN_SCALE, NEG_INF)
            s_ctx = _dot_nt(qh, kc) * ATTN_SCALE
            p_loc, p_ctx = _softmax_parts([s_loc, s_ctx], sink_ref[h])
            o_ref[:, h * HEAD_DIM:(h + 1) * HEAD_DIM] = (_dot(p_loc, vl) + _dot(p_ctx, vc)).astype(BF16)


def _attn_latent(proj, sink, cache_k, cache_v, layer, cos, sin, m_prompt, n_seq, t):
    kv_w = 2 * A_KV_WIDTH
    nb = t // ATTN_BLOCK
    past = cache_k.shape[2]
    row0 = m_prompt // ATTN_BLOCK
    seq0 = m_prompt // t
    return pl.pallas_call(
        functools.partial(_attn_lat_kernel, t=t),
        grid=(n_seq, nb),
        in_specs=[
            pl.BlockSpec(memory_space=pltpu.SMEM),
            pl.BlockSpec((ATTN_BLOCK, A_WIDTH), lambda b, n: (row0 + b * nb + n, CT_Q)),
            pl.BlockSpec((t, kv_w), lambda b, n: (seq0 + b, CT_MISC * COL_TILE // kv_w)),
            pl.BlockSpec((ATTN_BLOCK, A_WIDTH), lambda b, n: (n, 0)),
            pl.BlockSpec((ATTN_BLOCK, A_WIDTH), lambda b, n: (n, 0)),
            pl.BlockSpec((t, A_KV_WIDTH), lambda b, n: (0, 0)),
            pl.BlockSpec((t, A_KV_WIDTH), lambda b, n: (0, 0)),
            pl.BlockSpec((1, 1, past, A_KV_WIDTH), lambda b, n: (b, layer, 0, 0)),
            pl.BlockSpec((1, 1, past, A_KV_WIDTH), lambda b, n: (b, layer, 0, 0)),
        ],
        out_specs=pl.BlockSpec((ATTN_BLOCK, A_WIDTH), lambda b, n: (b * nb + n, 0)),
        out_shape=jax.ShapeDtypeStruct((n_seq * t, A_WIDTH), BF16),
        scratch_shapes=[
            pltpu.VMEM((t, A_KV_WIDTH), BF16),
            pltpu.VMEM((t, A_KV_WIDTH), BF16),
            pltpu.VMEM((past, A_KV_WIDTH), BF16),
            pltpu.VMEM((past, A_KV_WIDTH), BF16),
        ],
        compiler_params=_params(("parallel", "arbitrary")),
        name="attn_latent",
    )(sink, proj, proj, cos, sin, cos, sin, cache_k, cache_v)


def _rope_tables(t):
    pos = jnp.arange(t)
    half = HEAD_DIM // 2
    freqs = ROPE_BASE ** (-jnp.arange(0, half, 2, dtype=F32) / half)

    def tab(p):
        ang = p.astype(F32)[:, None] * freqs[None, :]
        c, s = jnp.cos(ang), jnp.sin(ang)
        return jnp.concatenate([c, c], -1), jnp.concatenate([-s, s], -1)

    c_row, s_row = tab(pos // GRID_W)
    c_col, s_col = tab(pos % GRID_W)
    cos = jnp.concatenate([c_row, c_col], -1)
    sin = jnp.concatenate([s_row, s_col], -1)
    return jnp.tile(cos, (1, A_HEADS)), jnp.tile(sin, (1, A_HEADS))


def _sgu_kernel(u_ref, v_ref, ng_ref, w_ref, b_ref, o_ref):
    for ch in range(SGU_TM // SGU_CHUNK):
        rs = slice(ch * SGU_CHUNK, (ch + 1) * SGU_CHUNK)
        for g in range(SGU_GROUPS):
            cs = slice(g * LANES, (g + 1) * LANES)
            vn = _layer_norm(_gelu(v_ref[rs, cs])) * ng_ref[:, cs]
            mixed = _dot(w_ref[g], vn.astype(BF16)) + b_ref[g]
            o_ref[rs, cs] = (_gelu(u_ref[rs, cs]) * mixed).astype(BF16)


def _sgu(proj, norm_g, w_s, bias):
    m = proj.shape[0]
    return pl.pallas_call(
        _sgu_kernel,
        grid=(m // SGU_TM,),
        in_specs=[
            pl.BlockSpec((SGU_TM, SGU_WIDTH), lambda i: (i, CT_SGU_U)),
            pl.BlockSpec((SGU_TM, SGU_WIDTH), lambda i: (i, CT_SGU_V)),
            pl.BlockSpec((1, SGU_WIDTH), lambda i: (0, 0)),
            pl.BlockSpec((SGU_GROUPS, SGU_CHUNK, SGU_CHUNK), lambda i: (0, 0, 0)),
            pl.BlockSpec((SGU_GROUPS, SGU_CHUNK, LANES), lambda i: (0, 0, 0)),
        ],
        out_specs=pl.BlockSpec((SGU_TM, SGU_WIDTH), lambda i: (i, 0)),
        out_shape=jax.ShapeDtypeStruct((m, SGU_WIDTH), BF16),
        compiler_params=_params(("parallel",)),
        name="sgu",
    )(proj, proj, norm_g, w_s, bias)


def _head_sums(x, ones_bd):
    return _dot(x, ones_bd, precision=HIGHEST)


def _rwkv_pre_kernel(x_ref, xp_ref, xn_ref, lo_ref, lop_ref, lon_ref, mu_ref, mul_ref, kk_ref, ka_ref,
                     rk_ref, w0_ref, a0_ref, w2_ref, a2_ref, g2_ref, ones_ref,
                     r_ref, v_ref, kkn_ref, gate_ref, bonus_ref,
                     lw0_ref, b0_ref, k0_ref, lw1_ref, b1_ref, k1_ref, *, n_prompt_tiles, tiles_per_seq):
    i = pl.program_id(0)
    is_prompt = i < n_prompt_tiles
    j = lax.rem(jnp.maximum(i - n_prompt_tiles, 0), tiles_per_seq)
    first = jnp.logical_or(is_prompt, j == 0)
    last = jnp.logical_or(is_prompt, j == tiles_per_seq - 1)

    def shift(x, xp, xn, mu):
        tm = x.shape[0]
        row = lax.broadcasted_iota(jnp.int32, x.shape, 0)
        prev_row = jnp.where(first, 0.0, xp[7:8, :])
        next_row = jnp.where(last, 0.0, xn[0:1, :])
        prev = jnp.where(row == 0, prev_row, pltpu.roll(x, 1, 0))
        nxt = jnp.where(row == tm - 1, next_row, pltpu.roll(x, tm - 1, 0))
        return x + mu[0:1, :] * (prev - x) + mu[1:2, :] * (nxt - x)

    xs = shift(x_ref[...], xp_ref[...], xn_ref[...], mu_ref[...])
    lo = shift(lo_ref[...], lop_ref[...], lon_ref[...], mul_ref[...])
    r = xs[:, :C_WIDTH]
    k = xs[:, C_WIDTH:2 * C_WIDTH]
    v = xs[:, 2 * C_WIDTH:]
    tw = jnp.tanh(lo[:, :W_LORA]).astype(BF16)
    al = lo[:, W_LORA:W_LORA + A_LORA].astype(BF16)
    gl = _sigmoid(lo[:, W_LORA + A_LORA:]).astype(BF16)
    ones_bd = ones_ref[...]

    r_ref[...] = r
    v_ref[...] = v
    gate_ref[...] = _dot(gl, g2_ref[...])
    kk0 = k * kk_ref[...]
    for p in range(HEAD_PAIRS):
        cs = slice(p * LANES, (p + 1) * LANES)
        ss = _head_sums(kk0[:, cs] * kk0[:, cs], ones_bd)
        kkn_ref[:, cs] = kk0[:, cs] * lax.rsqrt(jnp.maximum(ss, 1e-24))
    kkn = kkn_ref[...]

    bonus = None
    for d, (lw_ref, b_ref, kd_ref) in enumerate(((lw0_ref, b0_ref, k0_ref), (lw1_ref, b1_ref, k1_ref))):
        z = w0_ref[d:d + 1, :] + _dot(tw, w2_ref[d])
        softplus_neg = jnp.maximum(-z, 0.0) + jnp.log(1.0 + jnp.exp(-jnp.abs(z)))
        lw_ref[...] = -jnp.exp(-softplus_neg - 0.5)
        a = _sigmoid(a0_ref[d:d + 1, :] + _dot(al, a2_ref[d]))
        kd = k * (1.0 + (a - 1.0) * ka_ref[...])
        kd_ref[...] = kd
        b_ref[...] = kkn * a
        rkr = r * kd * rk_ref[...]
        parts = [_head_sums(rkr[:, p * LANES:(p + 1) * LANES], ones_bd) for p in range(HEAD_PAIRS)]
        term = jnp.concatenate(parts, axis=1) * v
        bonus = term if bonus is None else bonus + term
    bonus_ref[...] = bonus


def _rwkv_pre(proj, lp, ones_bd, m_prompt, t_sample):
    m = proj.shape[0]
    tm = PRE_TM
    rkv_w = 3 * C_WIDTH
    lo_blk = (CT_MISC * COL_TILE + 2 * A_KV_WIDTH) // LORA_COLS
    n8 = m // 8
    prev8 = lambda i: jnp.maximum(i * (tm // 8) - 1, 0)
    next8 = lambda i: jnp.minimum((i + 1) * (tm // 8), n8 - 1)
    full = lambda shape: pl.BlockSpec(shape, lambda i: (0,) * len(shape))
    out_spec = pl.BlockSpec((tm, C_WIDTH), lambda i: (i, 0))
    out_shape = jax.ShapeDtypeStruct((m, C_WIDTH), F32)
    kern = functools.partial(_rwkv_pre_kernel, n_prompt_tiles=m_prompt // tm, tiles_per_seq=t_sample // tm)
    return pl.pallas_call(
        kern,
        grid=(m // tm,),
        in_specs=[
            pl.BlockSpec((tm, rkv_w), lambda i: (i, CT_RKV * COL_TILE // rkv_w)),
            pl.BlockSpec((8, rkv_w), lambda i: (prev8(i), CT_RKV * COL_TILE // rkv_w)),
            pl.BlockSpec((8, rkv_w), lambda i: (next8(i), CT_RKV * COL_TILE // rkv_w)),
            pl.BlockSpec((tm, LORA_COLS), lambda i: (i, lo_blk)),
            pl.BlockSpec((8, LORA_COLS), lambda i: (prev8(i), lo_blk)),
            pl.BlockSpec((8, LORA_COLS), lambda i: (next8(i), lo_blk)),
            full((2, rkv_w)), full((2, LORA_COLS)), full((1, C_WIDTH)), full((1, C_WIDTH)),
            full((1, C_WIDTH)), full((2, C_WIDTH)), full((2, C_WIDTH)),
            full((2, W_LORA, C_WIDTH)), full((2, A_LORA, C_WIDTH)), full((G_LORA, C_WIDTH)),
            full((LANES, LANES)),
        ],
        out_specs=[out_spec] * 11,
        out_shape=[out_shape] * 11,
        compiler_params=_params(("parallel",)),
        name="rwkv_pre",
    )(proj, proj, proj, proj, proj, proj, lp["mu_rkv"], lp["mu_lora"], lp["k_k"], lp["k_a"], lp["r_k"],
      lp["w0"], lp["a0"], lp["w2"], lp["a2"], lp["g2"], ones_bd)


def _expand_heads(x, lane_lo):
    return jnp.concatenate([jnp.where(lane_lo, x, 0.0), jnp.where(lane_lo, 0.0, x)], axis=0).astype(BF16)


def _scan_masks(reverse):
    c = SCAN_CHUNK
    ri = lax.broadcasted_iota(jnp.int32, (2 * c, 2 * c), 0)
    ci = lax.broadcasted_iota(jnp.int32, (2 * c, 2 * c), 1)
    same = (ri // c) == (ci // c)
    tr, tc = ri % c, ci % c
    strict = jnp.logical_and(same, (tc > tr) if reverse else (tc < tr))
    incl = jnp.logical_and(same, (tc >= tr) if reverse else (tc <= tr))
    levels = []
    s = 1
    while s < c:
        late, early = (ci, ri) if reverse else (ri, ci)
        blk = jnp.logical_and((ri // (2 * s)) == (ci // (2 * s)),
                              jnp.logical_and(late % (2 * s) >= s, early % (2 * s) < s))
        levels.append(blk)
        s *= 2
    eye = ri == ci
    return strict, incl, levels, eye


def _scan_chunk(r, v, kk, lw, b, k, cum, state, masks, reverse):
    c = SCAN_CHUNK
    lane_lo = lax.broadcasted_iota(jnp.int32, (c, LANES), 1) < HEAD_DIM
    strict, incl, levels, eye = masks

    cum_x = cum - lw
    cum_end = cum[0:1, :] if reverse else cum[c - 1:c, :]
    inv = jnp.exp(-cum)
    tail = jnp.exp(cum_end - cum)
    ar = jnp.concatenate([_expand_heads(kk * jnp.exp(cum_x), lane_lo),
                          _expand_heads(r * jnp.exp(cum), lane_lo)], axis=0)
    bk = jnp.concatenate([_expand_heads(b * inv, lane_lo), _expand_heads(k * inv, lane_lo)], axis=0)
    bk_tail = jnp.concatenate([_expand_heads(b * tail, lane_lo), _expand_heads(k * tail, lane_lo)], axis=0)
    ve = _expand_heads(v, lane_lo)

    g = _dot_nt(ar, bk)
    n = 2 * c
    g_ab = g[:n, :n]
    d_ak = jnp.where(strict, g[:n, n:], 0.0).astype(BF16)
    d_r = jnp.concatenate([jnp.where(incl, g[n:, :n], 0.0), jnp.where(incl, g[n:, n:], 0.0)], axis=1)
    x = _dot_nt(ar, state.astype(BF16))

    t_inv = jnp.where(levels[0], -g_ab, jnp.where(eye, 1.0, 0.0))
    for blk in levels[1:]:
        t_b = t_inv.astype(BF16)
        t_inv = t_inv - _dot(_dot(t_b, jnp.where(blk, g_ab, 0.0).astype(BF16)).astype(BF16), t_b)
    u = -_dot(t_inv.astype(BF16), (x[:n] + _dot(d_ak, ve)).astype(BF16))
    uv = jnp.concatenate([u.astype(BF16), ve], axis=0)
    ye = x[n:] + _dot(d_r.astype(BF16), uv)
    y = ye[:c] + ye[c:]
    new_state = state * jnp.exp(cum_end) + _dot_tn(uv, bk_tail)
    return y, new_state


def _scan_kernel(*refs, zero_init):
    if zero_init:
        fwd, bwd, (tri_f, tri_b), (yf_ref, yb_ref, s_ref) = refs[0:6], refs[6:12], refs[12:14], refs[14:17]
    else:
        fwd, bwd, s0_ref = refs[0:6], refs[6:12], refs[12]
        (tri_f, tri_b), (yf_ref, yb_ref, s_ref) = refs[13:15], refs[15:18]

    @pl.when(pl.program_id(1) == 0)
    def _():
        if zero_init:
            s_ref[...] = jnp.zeros(s_ref.shape, F32)
        else:
            s_ref[...] = s0_ref[...]

    for d, (ins, tri, y_ref) in enumerate(((fwd, tri_f, yf_ref), (bwd, tri_b, yb_ref))):
        r_ref, v_ref, kk_ref, lw_ref, b_ref, k_ref = ins
        cum_all = _dot(tri[...], lw_ref[...], precision=HIGHEST)
        masks = _scan_masks(reverse=(d == 1))
        for p in range(HEAD_PAIRS):
            cs = slice(p * LANES, (p + 1) * LANES)
            y, new_state = _scan_chunk(r_ref[:, cs], v_ref[:, cs], kk_ref[:, cs], lw_ref[:, cs], b_ref[:, cs],
                                       k_ref[:, cs], cum_all[:, cs], s_ref[0, d, p], masks, reverse=(d == 1))
            y_ref[:, cs] = y
            s_ref[0, d, p] = new_state


def _rwkv_scan(pre, s0, row0, n_seq, t):
    r, v, kk, _, _, lw0, b0, k0, lw1, b1, k1 = pre
    c = SCAN_CHUNK
    nc = t // c
    blk0 = row0 // c
    fspec = pl.BlockSpec((c, C_WIDTH), lambda s, j: (blk0 + s * nc + j, 0))
    bspec = pl.BlockSpec((c, C_WIDTH), lambda s, j: (blk0 + s * nc + nc - 1 - j, 0))
    sspec = pl.BlockSpec((1, 2, HEAD_PAIRS, LANES, LANES), lambda s, j: (s, 0, 0, 0, 0))
    tspec = pl.BlockSpec((c, c), lambda s, j: (0, 0))
    tri_f = jnp.tril(jnp.ones((c, c), F32))
    args = [r, v, kk, lw0, b0, k0, r, v, kk, lw1, b1, k1]
    in_specs = [fspec] * 6 + [bspec] * 6
    if s0 is not None:
        args.append(s0)
        in_specs.append(sspec)
    args += [tri_f, tri_f.T]
    in_specs += [tspec, tspec]
    yf, yb, s_fin = pl.pallas_call(
        functools.partial(_scan_kernel, zero_init=s0 is None),
        grid=(n_seq, nc),
        in_specs=in_specs,
        out_specs=[
            pl.BlockSpec((c, C_WIDTH), lambda s, j: (s * nc + j, 0)),
            pl.BlockSpec((c, C_WIDTH), lambda s, j: (s * nc + nc - 1 - j, 0)),
            sspec,
        ],
        out_shape=[
            jax.ShapeDtypeStruct((n_seq * t, C_WIDTH), F32),
            jax.ShapeDtypeStruct((n_seq * t, C_WIDTH), F32),
            jax.ShapeDtypeStruct((n_seq, 2, HEAD_PAIRS, LANES, LANES), F32),
        ],
        compiler_params=_params(("parallel", "arbitrary")),
        name="rwkv_scan",
    )(*args)
    return yf, yb, s_fin


def _rwkv_post_kernel(yf_ref, yb_ref, bonus_ref, gate_ref, g_ref, b_ref, ones_ref, o_ref):
    ones_bd = ones_ref[...]
    for p in range(HEAD_PAIRS):
        cs = slice(p * LANES, (p + 1) * LANES)
        y = yf_ref[:, cs] + yb_ref[:, cs]
        mu = _head_sums(y, ones_bd) * (1.0 / HEAD_DIM)
        yc = y - mu
        var = _head_sums(yc * yc, ones_bd) * (1.0 / HEAD_DIM)
        yn = yc * lax.rsqrt(var + GN_EPS) * g_ref[:, cs] + b_ref[:, cs]
        o_ref[:, cs] = ((yn + bonus_ref[:, cs]) * gate_ref[:, cs]).astype(BF16)


def _rwkv_post(yf, yb, bonus, gate, ln_g, ln_b, ones_bd):
    m = yf.shape[0]
    spec = pl.BlockSpec((POST_TM, C_WIDTH), lambda i: (i, 0))
    vec = pl.BlockSpec((1, C_WIDTH), lambda i: (0, 0))
    return pl.pallas_call(
        _rwkv_post_kernel,
        grid=(m // POST_TM,),
        in_specs=[spec, spec, spec, spec, vec, vec, pl.BlockSpec((LANES, LANES), lambda i: (0, 0))],
        out_specs=spec,
        out_shape=jax.ShapeDtypeStruct((m, C_WIDTH), BF16),
        compiler_params=_params(("parallel",)),
        name="rwkv_post",
    )(yf, yb, bonus, gate, ln_g, ln_b, ones_bd)


def _merge_kernel(x_ref, attn_ref, sgu_ref, rw_ref, ga_ref, gb_ref, gc_ref, bg_ref, g1_ref,
                  wa_ref, wb_ref, wc_ref, wo_ref, lg_ref, lb_ref, o_ref):
    merged = None
    for br, (a_ref, w_ref, g_ref) in enumerate(((attn_ref, wa_ref, ga_ref), (sgu_ref, wb_ref, gb_ref),
                                                 (rw_ref, wc_ref, gc_ref))):
        gate = _sigmoid(g_ref[...] + bg_ref[:, br * D_MODEL:(br + 1) * D_MODEL])
        term = gate * _dot(a_ref[...], w_ref[...])
        merged = term if merged is None else merged + term
    out = _dot(merged.astype(BF16), wo_ref[...])
    y = _layer_norm(ALPHA * x_ref[...] + g1_ref[0] * out)
    o_ref[...] = y * lg_ref[...] + lb_ref[...]


def _merge(x, attn, sgu, rw, proj, mod3, lp, m_prompt, t_sample):
    m = x.shape[0]
    tm = MERGE_TM
    req = functools.partial(_request_of_tile, tm=tm, m_prompt=m_prompt, t_sample=t_sample)
    row = lambda w: pl.BlockSpec((tm, w), lambda i: (i, 0))
    const = lambda shape: pl.BlockSpec(shape, lambda i: (0,) * len(shape), pipeline_mode=pl.Buffered(1))
    gate = lambda br: pl.BlockSpec((tm, D_MODEL), lambda i: (i, CT_GATE * COL_TILE // D_MODEL + br))
    return pl.pallas_call(
        _merge_kernel,
        grid=(m // tm,),
        in_specs=[
            row(D_MODEL), row(A_WIDTH), row(SGU_WIDTH), row(C_WIDTH), gate(0), gate(1), gate(2),
            const((1, N_BRANCH * D_MODEL)),
            pl.BlockSpec((1, 1, D_MODEL), lambda i: (req(i) * 6 + 2, 0, 0)),
            const((A_WIDTH, D_MODEL)), const((SGU_WIDTH, D_MODEL)), const((C_WIDTH, D_MODEL)),
            const((D_MODEL, D_MODEL)), const((1, D_MODEL)), const((1, D_MODEL)),
        ],
        out_specs=row(D_MODEL),
        out_shape=jax.ShapeDtypeStruct((m, D_MODEL), F32),
        compiler_params=_params(("parallel",)),
        name="merge",
    )(x, attn, sgu, rw, proj, proj, proj, lp["b_gate"], mod3, lp["w_branch_a"], lp["w_branch_b"],
      lp["w_branch_c"], lp["w_out"], lp["ln1_g"], lp["ln1_b"])


def _ffn_kernel(x_ref, sh_ref, sc_ref, g2_ref, wg_ref, wu_ref, wo_ref, lg_ref, lb_ref, o_ref, h_scr, acc_scr):
    f = pl.program_id(1)

    @pl.when(f == 0)
    def _():
        h_scr[...] = (_layer_norm(x_ref[...]) * (1.0 + sc_ref[0]) + sh_ref[0]).astype(BF16)
        acc_scr[...] = jnp.zeros(acc_scr.shape, F32)

    h = h_scr[...]
    g = _dot(h, wg_ref[...])
    u = _dot(h, wu_ref[...])
    acc_scr[...] += _dot((g * _sigmoid(g) * u).astype(BF16), wo_ref[...])

    @pl.when(f == pl.num_programs(1) - 1)
    def _():
        y = _layer_norm(ALPHA * x_ref[...] + g2_ref[0] * acc_scr[...])
        o_ref[...] = y * lg_ref[...] + lb_ref[...]


def _ffn_dense(x, mod3, w_in, w_out, ln_g, ln_b, m_prompt, t_sample):
    m = x.shape[0]
    tm, tf = FFN_TM, FFN_TF
    nf = D_FF // tf
    req = functools.partial(_request_of_tile, tm=tm, m_prompt=m_prompt, t_sample=t_sample)
    modspec = lambda k: pl.BlockSpec((1, 1, D_MODEL), lambda i, f: (req(i) * 6 + k, 0, 0))
    vec = pl.BlockSpec((1, D_MODEL), lambda i, f: (0, 0))
    return pl.pallas_call(
        _ffn_kernel,
        grid=(m // tm, nf),
        in_specs=[
            pl.BlockSpec((tm, D_MODEL), lambda i, f: (i, 0)),
            modspec(3), modspec(4), modspec(5),
            pl.BlockSpec((D_MODEL, tf), lambda i, f: (0, f)),
            pl.BlockSpec((D_MODEL, tf), lambda i, f: (0, nf + f)),
            pl.BlockSpec((tf, D_MODEL), lambda i, f: (f, 0)),
            vec, vec,
        ],
        out_specs=pl.BlockSpec((tm, D_MODEL), lambda i, f: (i, 0)),
        out_shape=jax.ShapeDtypeStruct((m, D_MODEL), F32),
        scratch_shapes=[pltpu.VMEM((tm, D_MODEL), BF16), pltpu.VMEM((tm, D_MODEL), F32)],
        compiler_params=_params(("parallel", "arbitrary")),
        name="ffn_dense",
    )(x, mod3, mod3, mod3, w_in, w_in, w_out, ln_g, ln_b)


def _route_kernel(x_ref, sh_ref, sc_ref, wr_ref, h_ref, idx_ref, gate_ref):
    h = _layer_norm(x_ref[...]) * (1.0 + sc_ref[0]) + sh_ref[0]
    h_ref[...] = h
    logits = _dot(h, wr_ref[...], precision=HIGHEST)
    lane = lax.broadcasted_iota(jnp.int32, logits.shape, 1)
    logits = jnp.where(lane < N_EXPERTS, logits, -jnp.inf)
    lane_f = lane.astype(F32)
    m1 = jnp.max(logits, axis=-1, keepdims=True)
    i1 = jnp.min(jnp.where(logits == m1, lane_f, float(LANES)), axis=-1, keepdims=True)
    rest = jnp.where(lane_f == i1, -jnp.inf, logits)
    m2 = jnp.max(rest, axis=-1, keepdims=True)
    i2 = jnp.min(jnp.where(rest == m2, lane_f, float(LANES)), axis=-1, keepdims=True)
    e = jnp.exp(m2 - m1)
    g1 = 1.0 / (1.0 + e)
    idx_ref[...] = jnp.where(lane == 0, i1, jnp.where(lane == 1, i2, 0.0)).astype(jnp.int32)
    gate_ref[...] = jnp.where(lane == 0, g1, jnp.where(lane == 1, e * g1, 0.0))


def _route(x, mod3, router_p, m_prompt, t_sample):
    m = x.shape[0]
    tm = ROUTE_TM
    req = functools.partial(_request_of_tile, tm=tm, m_prompt=m_prompt, t_sample=t_sample)
    modspec = lambda k: pl.BlockSpec((1, 1, D_MODEL), lambda i: (req(i) * 6 + k, 0, 0))
    return pl.pallas_call(
        _route_kernel,
        grid=(m // tm,),
        in_specs=[pl.BlockSpec((tm, D_MODEL), lambda i: (i, 0)), modspec(3), modspec(4),
                  pl.BlockSpec((D_MODEL, LANES), lambda i: (0, 0))],
        out_specs=[pl.BlockSpec((tm, D_MODEL), lambda i: (i, 0)), pl.BlockSpec((tm, LANES), lambda i: (i, 0)),
                   pl.BlockSpec((tm, LANES), lambda i: (i, 0))],
        out_shape=[jax.ShapeDtypeStruct((m, D_MODEL), F32), jax.ShapeDtypeStruct((m, LANES), jnp.int32),
                   jax.ShapeDtypeStruct((m, LANES), F32)],
        compiler_params=_params(("parallel",)),
        name="route",
    )(x, mod3, mod3, router_p)


def _row_copy(src_hbm, row, dst, dst_row, sem):
    return pltpu.make_async_copy(src_hbm.at[pl.ds(row, 1)], dst.at[pl.ds(dst_row, 1)], sem)


def _moe_kernel(tile_e_ref, tile_valid_ref, row_tok_ref, h_hbm, wg_ref, wu_ref, wo_ref, o_ref, x_scr, sem):
    i = pl.program_id(0)
    f = pl.program_id(1)
    valid = tile_valid_ref[i]

    @pl.when(f == 0)
    def _():
        o_ref[...] = jnp.zeros(o_ref.shape, F32)

        @pl.when(valid > 0)
        def _():
            def start(r, carry):
                _row_copy(h_hbm, row_tok_ref[0, 0, r], x_scr, r, sem).start()
                return carry

            def wait(r, carry):
                _row_copy(h_hbm, 0, x_scr, r, sem).wait()
                return carry

            lax.fori_loop(0, MOE_TM, start, 0)
            lax.fori_loop(0, MOE_TM, wait, 0)

    @pl.when(valid > 0)
    def _():
        wg = wg_ref[0].astype(BF16)
        wu = wu_ref[0].astype(BF16)
        wo = wo_ref[0].astype(BF16)
        for sb in range(MOE_TM // MOE_SUB):
            @pl.when(sb * MOE_SUB < valid)
            def _():
                rs = slice(sb * MOE_SUB, (sb + 1) * MOE_SUB)
                x = x_scr[rs, :].astype(BF16)
                g = _dot(x, wg)
                u = _dot(x, wu)
                o_ref[rs, :] += _dot((g * _sigmoid(g) * u).astype(BF16), wo)


def _moe_experts(h, row_tok, tile_e, tile_valid, w_in, w_out):
    n_tiles = tile_e.shape[0]
    nf = D_FF_EXPERT // MOE_TF
    fidx = lambda i, f, tv: jnp.where(tv[i] > 0, f, nf - 1)
    grid_spec = pltpu.PrefetchScalarGridSpec(
        num_scalar_prefetch=2,
        grid=(n_tiles, nf),
        in_specs=[
            pl.BlockSpec((1, 1, MOE_TM), lambda i, f, te, tv: (i, 0, 0), memory_space=pltpu.SMEM),
            pl.BlockSpec(memory_space=pl.ANY),
            pl.BlockSpec((1, D_MODEL, MOE_TF), lambda i, f, te, tv: (te[i], 0, fidx(i, f, tv))),
            pl.BlockSpec((1, D_MODEL, MOE_TF), lambda i, f, te, tv: (te[i], 0, nf + fidx(i, f, tv))),
            pl.BlockSpec((1, MOE_TF, D_MODEL), lambda i, f, te, tv: (te[i], fidx(i, f, tv), 0)),
        ],
        out_specs=pl.BlockSpec((MOE_TM, D_MODEL), lambda i, f, te, tv: (i, 0)),
        scratch_shapes=[pltpu.VMEM((MOE_TM, D_MODEL), F32), pltpu.SemaphoreType.DMA(())],
    )
    return pl.pallas_call(
        _moe_kernel,
        grid_spec=grid_spec,
        out_shape=jax.ShapeDtypeStruct((n_tiles * MOE_TM, D_MODEL), F32),
        compiler_params=_params(("arbitrary", "arbitrary")),
        name="moe_experts",
    )(tile_e, tile_valid, row_tok.reshape(n_tiles, 1, MOE_TM), h, w_in, w_in, w_out)


def _combine_kernel(pos_ref, ys_hbm, x_ref, gate_ref, g2_ref, lg_ref, lb_ref, o_ref, buf, sem):
    tm = COMB_TM

    def start(t, carry):
        for k in range(TOP_K):
            _row_copy(ys_hbm, pos_ref[0, 0, t * TOP_K + k], buf.at[k], t, sem).start()
        return carry

    def wait(t, carry):
        for k in range(TOP_K):
            _row_copy(ys_hbm, 0, buf.at[k], t, sem).wait()
        return carry

    lax.fori_loop(0, tm, start, 0)
    lax.fori_loop(0, tm, wait, 0)
    y = gate_ref[:, 0:1] * buf[0] + gate_ref[:, 1:2] * buf[1]
    z = _layer_norm(ALPHA * x_ref[...] + g2_ref[0] * y)
    o_ref[...] = z * lg_ref[...] + lb_ref[...]


def _moe_combine(x, ys, pos, gates, mod3, ln_g, ln_b, m_prompt, t_sample):
    m = x.shape[0]
    tm = COMB_TM
    req = functools.partial(_request_of_tile, tm=tm, m_prompt=m_prompt, t_sample=t_sample)
    return pl.pallas_call(
        _combine_kernel,
        grid=(m // tm,),
        in_specs=[
            pl.BlockSpec((1, 1, tm * TOP_K), lambda i: (i, 0, 0), memory_space=pltpu.SMEM),
            pl.BlockSpec(memory_space=pl.ANY),
            pl.BlockSpec((tm, D_MODEL), lambda i: (i, 0)),
            pl.BlockSpec((tm, LANES), lambda i: (i, 0)),
            pl.BlockSpec((1, 1, D_MODEL), lambda i: (req(i) * 6 + 5, 0, 0)),
            pl.BlockSpec((1, D_MODEL), lambda i: (0, 0)),
            pl.BlockSpec((1, D_MODEL), lambda i: (0, 0)),
        ],
        out_specs=pl.BlockSpec((tm, D_MODEL), lambda i: (i, 0)),
        out_shape=jax.ShapeDtypeStruct((m, D_MODEL), F32),
        scratch_shapes=[pltpu.VMEM((TOP_K, tm, D_MODEL), F32), pltpu.SemaphoreType.DMA(())],
        compiler_params=_params(("arbitrary",)),
        name="moe_combine",
    )(pos.reshape(m // tm, 1, tm * TOP_K), ys, x, gates, mod3, ln_g, ln_b)


def _moe_plan(idx):
    m = idx.shape[0]
    flat_e = idx.reshape(-1)
    n_assign = m * TOP_K
    onehot = (flat_e[:, None] == jnp.arange(N_EXPERTS, dtype=jnp.int32)[None, :]).astype(jnp.int32)
    csum = jnp.cumsum(onehot, axis=0)
    counts = csum[-1]
    rank = jnp.take_along_axis(csum, flat_e[:, None], axis=1)[:, 0] - 1
    padded = (counts + MOE_TM - 1) // MOE_TM * MOE_TM
    pad_end = jnp.cumsum(padded)
    pad_start = pad_end - padded
    pos = (pad_start[flat_e] + rank).astype(jnp.int32)
    n_tiles = n_assign // MOE_TM + N_EXPERTS
    row_tok = jnp.zeros((n_tiles * MOE_TM,), jnp.int32).at[pos].set(jnp.arange(n_assign, dtype=jnp.int32) // TOP_K)
    tile_start = jnp.arange(n_tiles, dtype=jnp.int32) * MOE_TM
    tile_e = jnp.searchsorted(pad_end, tile_start, side="right").astype(jnp.int32)
    used = tile_e < N_EXPERTS
    last_e = jnp.max(jnp.where(counts > 0, jnp.arange(N_EXPERTS, dtype=jnp.int32), 0))
    tile_e = jnp.where(used, tile_e, last_e)
    seg_end = pad_start[tile_e] + counts[tile_e]
    tile_valid = jnp.where(used, jnp.clip(seg_end - tile_start, 0, MOE_TM), 0).astype(jnp.int32)
    return row_tok, tile_e, tile_valid, pos


def _reorder_w_in(w_in):
    o_q, o_k, o_v = 0, A_WIDTH, A_WIDTH + A_KV_WIDTH
    o_xb = o_v + A_KV_WIDTH
    o_xc = o_xb + 2 * SGU_WIDTH
    o_lora = o_xc + 3 * C_WIDTH
    o_gl = o_xc + C_COLS
    cols = lambda a, n: w_in[:, :, a:a + n]
    pad = jnp.zeros(w_in.shape[:2] + (COL_TILE - 2 * A_KV_WIDTH - LORA_COLS,), w_in.dtype)
    parts = [cols(o_gl, N_BRANCH * D_MODEL), cols(o_xb, 2 * SGU_WIDTH), cols(o_q, A_WIDTH),
             cols(o_xc, 3 * C_WIDTH), cols(o_k, 2 * A_KV_WIDTH), cols(o_lora, LORA_COLS), pad]
    return jnp.concatenate(parts, axis=-1).astype(BF16)


def kernel(x_prompt, x_sample, cache_k, cache_v, state_rwkv, c, c_ctx, w_mod, b_mod, w_in, b_gate, attn_sink, sgu_norm_g, sgu_w, sgu_b, rwkv_mu, rwkv_w0, rwkv_w2, rwkv_a0, rwkv_a2, rwkv_g2, rwkv_k_k, rwkv_k_a, rwkv_r_k, rwkv_ln_g, rwkv_ln_b, w_branch_a, w_branch_b, w_branch_c, w_out, ln1_g, ln1_b, ln2_g, ln2_b, ffn_w_in, ffn_w_out, moe_router, moe_w_in, moe_w_out):
    bp, tp, _ = x_prompt.shape
    bs, ts, _ = x_sample.shape
    past = cache_k.shape[2]
    mp, ms = bp * tp, bs * ts
    assert bs + 1 <= 16 and mp % IN_TM == 0 and ts % IN_TM == 0 and tp % PRE_TM == 0 and IN_TM % tp == 0

    x = jnp.concatenate([x_prompt.reshape(mp, D_MODEL), x_sample.reshape(ms, D_MODEL)], axis=0)
    cond = jnp.zeros((16, D_MODEL), F32).at[0].set(c_ctx).at[1:1 + bs].set(c)
    mod = _modulation(cond, w_mod, b_mod)

    w_in_p = _reorder_w_in(w_in)
    cos, sin = _rope_tables(ts)
    ones_bd = jnp.kron(jnp.eye(LANES // HEAD_DIM, dtype=F32), jnp.ones((HEAD_DIM, HEAD_DIM), F32))
    cache_k4 = cache_k.reshape(bs, DEPTH, past, A_KV_WIDTH)
    cache_v4 = cache_v.reshape(bs, DEPTH, past, A_KV_WIDTH)
    eye2 = jnp.eye(2, dtype=F32)

    new_k, new_v, new_s = [], [], []
    for l in range(DEPTH):
        mod3 = mod[l].reshape(16 * 6, 1, D_MODEL)
        lp = {
            "mu_rkv": rwkv_mu[l][:, :3 * C_WIDTH], "mu_lora": rwkv_mu[l][:, 3 * C_WIDTH:],
            "k_k": rwkv_k_k[l].reshape(1, C_WIDTH), "k_a": rwkv_k_a[l].reshape(1, C_WIDTH),
            "r_k": rwkv_r_k[l].reshape(1, C_WIDTH), "w0": rwkv_w0[l], "a0": rwkv_a0[l],
            "w2": rwkv_w2[l].astype(BF16), "a2": rwkv_a2[l].astype(BF16), "g2": rwkv_g2[l].astype(BF16),
            "b_gate": b_gate[l].reshape(1, N_BRANCH * D_MODEL),
            "w_branch_a": w_branch_a[l].astype(BF16), "w_branch_b": w_branch_b[l].astype(BF16),
            "w_branch_c": w_branch_c[l].astype(BF16), "w_out": w_out[l].astype(BF16),
            "ln1_g": ln1_g[l].reshape(1, D_MODEL), "ln1_b": ln1_b[l].reshape(1, D_MODEL),
        }
        proj = _in_proj(x, mod3, w_in_p[l], mp, ts)

        attn = jnp.concatenate([
            _attn_context(proj, attn_sink[l], bp, tp),
            _attn_latent(proj, attn_sink[l], cache_k4, cache_v4, l, cos, sin, mp, bs, ts),
        ], axis=0)
        kv_cols = proj[:mp, CT_MISC * COL_TILE:CT_MISC * COL_TILE + 2 * A_KV_WIDTH]
        new_k.append(kv_cols[:, :A_KV_WIDTH].reshape(bp, tp, A_KV_HEADS, HEAD_DIM))
        new_v.append(kv_cols[:, A_KV_WIDTH:].reshape(bp, tp, A_KV_HEADS, HEAD_DIM))

        sgu_bias = jnp.broadcast_to(sgu_b[l][:, :, None], (SGU_GROUPS, SGU_CHUNK, LANES))
        sgu = _sgu(proj, sgu_norm_g[l].reshape(1, SGU_WIDTH), sgu_w[l].astype(BF16), sgu_bias)

        pre = _rwkv_pre(proj, lp, ones_bd, mp, ts)
        s0 = state_rwkv[:, l].astype(F32).reshape(bs, 2, HEAD_PAIRS, 2, HEAD_DIM, HEAD_DIM)
        s0 = jnp.einsum("bdphij,hg->bdphigj", s0, eye2).reshape(bs, 2, HEAD_PAIRS, LANES, LANES)
        yf_p, yb_p, s_fin = _rwkv_scan(pre, None, 0, bp, tp)
        yf_s, yb_s, _ = _rwkv_scan(pre, s0, mp, bs, ts)
        rw = _rwkv_post(jnp.concatenate([yf_p, yf_s], 0), jnp.concatenate([yb_p, yb_s], 0), pre[4], pre[3],
                        rwkv_ln_g[l].reshape(1, C_WIDTH), rwkv_ln_b[l].reshape(1, C_WIDTH), ones_bd)
        s_fin = s_fin.reshape(bp, 2, HEAD_PAIRS, 2, HEAD_DIM, 2, HEAD_DIM)
        s_fin = jnp.stack([s_fin[:, :, :, 0, :, 0, :], s_fin[:, :, :, 1, :, 1, :]], axis=3)
        new_s.append(s_fin.reshape(bp, 2, C_HEADS, HEAD_DIM, HEAD_DIM))

        x1 = _merge(x, attn, sgu, rw, proj, mod3, lp, mp, ts)

        ln2g, ln2b = ln2_g[l].reshape(1, D_MODEL), ln2_b[l].reshape(1, D_MODEL)
        if l % 2 == 0:
            x = _ffn_dense(x1, mod3, ffn_w_in[l // 2].astype(BF16), ffn_w_out[l // 2].astype(BF16), ln2g, ln2b, mp, ts)
        else:
            router_p = jnp.zeros((D_MODEL, LANES), F32).at[:, :N_EXPERTS].set(moe_router[l // 2])
            h, idx, gates = _route(x1, mod3, router_p, mp, ts)
            row_tok, tile_e, tile_valid, pos = _moe_plan(idx[:, :TOP_K])
            ys = _moe_experts(h, row_tok, tile_e, tile_valid, moe_w_in[l // 2], moe_w_out[l // 2])
            x = _moe_combine(x1, ys, pos, gates, mod3, ln2g, ln2b, mp, ts)

    y_prompt = x[:mp].reshape(bp, tp, D_MODEL)
    y_sample = x[mp:].reshape(bs, ts, D_MODEL)
    return (y_prompt, y_sample, jnp.stack(new_k, axis=1), jnp.stack(new_v, axis=1), jnp.stack(new_s, axis=1))
```

```python
import functools

import jax
import jax.numpy as jnp
from jax import lax
from jax.experimental import pallas as pl
from jax.experimental.pallas import tpu as pltpu

F32 = jnp.float32
BF16 = jnp.bfloat16
HIGHEST = lax.Precision.HIGHEST

D_MODEL = 2048
DEPTH = 2
GRID_W = 64
A_HEADS = 16
A_KV_HEADS = 4
A_GROUPS = A_HEADS // A_KV_HEADS
HEAD_DIM = 64
A_WIDTH = A_HEADS * HEAD_DIM
A_KV_WIDTH = A_KV_HEADS * HEAD_DIM
ATTN_SCALE = HEAD_DIM ** -0.5
WINDOW = 128
ATTN_BLOCK = 128
ROPE_BASE = 10000.0
SGU_CHUNK = 128
SGU_GROUPS = 8
SGU_WIDTH = 1024
C_HEADS = 16
C_WIDTH = C_HEADS * HEAD_DIM
W_LORA = 64
A_LORA = 64
G_LORA = 128
LORA_COLS = W_LORA + A_LORA + G_LORA
C_COLS = 3 * C_WIDTH + LORA_COLS
N_BRANCH = 3
D_FF = 5632
N_EXPERTS = 8
TOP_K = 2
D_FF_EXPERT = 7168
ALPHA = (2 * DEPTH) ** 0.25
LN_EPS = 1e-6
GN_EPS = HEAD_DIM * 1e-5
NEG_INF = -1e30

LANES = 128
HEAD_PAIRS = C_HEADS * HEAD_DIM // LANES
SCAN_CHUNK = 64
VMEM_LIMIT = 56 * 2 ** 20

COL_TILE = 1024
N_COL_TILES = 13
PROJ_COLS = N_COL_TILES * COL_TILE
CT_GATE, CT_SGU_U, CT_SGU_V, CT_Q, CT_RKV, CT_MISC = 0, 6, 7, 8, 9, 12

IN_TM = 1024
MERGE_TM = 256
FFN_TM = 512
FFN_TF = 512
PRE_TM = 256
POST_TM = 512
SGU_TM = 512
ROUTE_TM = 512
MOE_TM = 1024
MOE_SUB = 256
MOE_TF = 512
COMB_TM = 256


def _params(sem):
    return pltpu.CompilerParams(dimension_semantics=sem, vmem_limit_bytes=VMEM_LIMIT)


def _dot(a, b, precision=None):
    return jnp.dot(a, b, preferred_element_type=F32, precision=precision)


def _dot_nt(a, b):
    return lax.dot_general(a, b, (((1,), (1,)), ((), ())), preferred_element_type=F32)


def _dot_tn(a, b):
    return lax.dot_general(a, b, (((0,), (0,)), ((), ())), preferred_element_type=F32)


def _layer_norm(x):
    mu = jnp.mean(x, axis=-1, keepdims=True)
    xc = x - mu
    var = jnp.mean(xc * xc, axis=-1, keepdims=True)
    return xc * lax.rsqrt(var + LN_EPS)


def _gelu(x):
    return 0.5 * x * (1.0 + jnp.tanh(0.7978845608028654 * (x + 0.044715 * (x * x * x))))


def _sigmoid(x):
    return 1.0 / (1.0 + jnp.exp(-x))


def _request_of_tile(i, tm, m_prompt, t_sample):
    n_prompt_tiles = m_prompt // tm
    return jnp.where(i < n_prompt_tiles, 0, 1 + (i - n_prompt_tiles) // (t_sample // tm))


def _mod_kernel(c_ref, w_ref, b_ref, o_ref):
    c = c_ref[...]
    h = (c * _sigmoid(c)).astype(BF16)
    o_ref[0] = _dot(h, w_ref[0].astype(BF16)) + b_ref[0]


def _modulation(cond, w_mod, b_mod):
    tn = 1024
    n = w_mod.shape[-1]
    return pl.pallas_call(
        _mod_kernel,
        grid=(DEPTH, n // tn),
        in_specs=[
            pl.BlockSpec((16, D_MODEL), lambda l, j: (0, 0)),
            pl.BlockSpec((1, D_MODEL, tn), lambda l, j: (l, 0, j)),
            pl.BlockSpec((1, 1, tn), lambda l, j: (l, 0, j)),
        ],
        out_specs=pl.BlockSpec((1, 16, tn), lambda l, j: (l, 0, j)),
        out_shape=jax.ShapeDtypeStruct((DEPTH, 16, n), F32),
        compiler_params=_params(("parallel", "parallel")),
        name="modulation",
    )(cond, w_mod, b_mod.reshape(DEPTH, 1, n))


def _in_kernel(x_ref, sh_ref, sc_ref, w_ref, o_ref, h_scr):
    @pl.when(pl.program_id(1) == 0)
    def _():
        y = _layer_norm(x_ref[...])
        h_scr[...] = (y * (1.0 + sc_ref[0]) + sh_ref[0]).astype(BF16)

    o_ref[...] = _dot(h_scr[...], w_ref[...])


def _in_proj(x, mod3, w_in_p, m_prompt, t_sample):
    m = x.shape[0]
    req = functools.partial(_request_of_tile, tm=IN_TM, m_prompt=m_prompt, t_sample=t_sample)
    return pl.pallas_call(
        _in_kernel,
        grid=(m // IN_TM, N_COL_TILES),
        in_specs=[
            pl.BlockSpec((IN_TM, D_MODEL), lambda i, j: (i, 0)),
            pl.BlockSpec((1, 1, D_MODEL), lambda i, j: (req(i) * 6 + 0, 0, 0)),
            pl.BlockSpec((1, 1, D_MODEL), lambda i, j: (req(i) * 6 + 1, 0, 0)),
            pl.BlockSpec((D_MODEL, COL_TILE), lambda i, j: (0, j)),
        ],
        out_specs=pl.BlockSpec((IN_TM, COL_TILE), lambda i, j: (i, j)),
        out_shape=jax.ShapeDtypeStruct((m, PROJ_COLS), F32),
        scratch_shapes=[pltpu.VMEM((IN_TM, D_MODEL), BF16)],
        compiler_params=_params(("parallel", "arbitrary")),
        name="in_proj",
    )(x, mod3, mod3, w_in_p)


def _softmax_parts(scores, sink):
    m = sink
    for s in scores:
        m = jnp.maximum(m, jnp.max(s, axis=-1, keepdims=True))
    ps = [jnp.exp(s - m) for s in scores]
    den = jnp.exp(sink - m)
    for p in ps:
        den = den + jnp.sum(p, axis=-1, keepdims=True)
    inv = 1.0 / den
    return [(p * inv).astype(BF16) for p in ps]


def _gqa_attention(q, key_sets, sink_ref, o_ref):
    rows = q.shape[0]
    col = lambda kvh: slice(kvh * HEAD_DIM, (kvh + 1) * HEAD_DIM)
    heads = lambda kvh: range(kvh * A_GROUPS, (kvh + 1) * A_GROUPS)
    masks = [m for _, _, m in key_sets]
    qs = [jnp.concatenate([q[:, h * HEAD_DIM:(h + 1) * HEAD_DIM] for h in heads(kvh)], axis=0).astype(BF16)
          for kvh in range(A_KV_HEADS)]
    scores = []
    for kvh in range(A_KV_HEADS):
        per_set = []
        for (k, _, _), m in zip(key_sets, masks):
            s = _dot_nt(qs[kvh], k[:, col(kvh)]) * ATTN_SCALE
            per_set.append(s if m is None else jnp.where(m, s, NEG_INF))
        scores.append(per_set)
    probs = []
    for kvh in range(A_KV_HEADS):
        sink = jnp.concatenate([jnp.full((rows, 1), sink_ref[h], F32) for h in heads(kvh)], axis=0)
        probs.append(_softmax_parts(scores[kvh], sink))
    for kvh in range(A_KV_HEADS):
        o = None
        for (_, v, _), p in zip(key_sets, probs[kvh]):
            term = _dot(p, v[:, col(kvh)])
            o = term if o is None else o + term
        for g, h in enumerate(heads(kvh)):
            o_ref[:, h * HEAD_DIM:(h + 1) * HEAD_DIM] = o[g * rows:(g + 1) * rows].astype(BF16)


def _attn_ctx_kernel(sink_ref, q_ref, kv_ref, o_ref):
    k = kv_ref[:, :A_KV_WIDTH].astype(BF16)
    v = kv_ref[:, A_KV_WIDTH:].astype(BF16)
    _gqa_attention(q_ref[...], [(k, v, None)], sink_ref, o_ref)


def _attn_context(proj, sink, n_seq, t):
    kv_w = 2 * A_KV_WIDTH
    return pl.pallas_call(
        _attn_ctx_kernel,
        grid=(n_seq,),
        in_specs=[
            pl.BlockSpec(memory_space=pltpu.SMEM),
            pl.BlockSpec((t, A_WIDTH), lambda b: (b, CT_Q)),
            pl.BlockSpec((t, kv_w), lambda b: (b, CT_MISC * COL_TILE // kv_w)),
        ],
        out_specs=pl.BlockSpec((t, A_WIDTH), lambda b: (b, 0)),
        out_shape=jax.ShapeDtypeStruct((n_seq * t, A_WIDTH), BF16),
        compiler_params=_params(("parallel",)),
        name="attn_context",
    )(sink, proj, proj)


def _rope(x, cos, sin_signed):
    w = x.shape[-1]
    lane = lax.broadcasted_iota(jnp.int32, x.shape, 1)
    swapped = jnp.where((lane % 32) < 16, pltpu.roll(x, w - 16, 1), pltpu.roll(x, 16, 1))
    return x * cos + swapped * sin_signed


def _attn_lat_kernel(sink_ref, q_ref, kv_ref, cq_ref, sq_ref, ck_ref, sk_ref, kc_ref, vc_ref, o_ref,
                     k_scr, v_scr, kc_scr, vc_scr, *, t):
    n = pl.program_id(1)
    span = ATTN_BLOCK + 2 * WINDOW

    @pl.when(n == 0)
    def _():
        k_scr[...] = _rope(kv_ref[:, :A_KV_WIDTH], ck_ref[...], sk_ref[...]).astype(BF16)
        v_scr[...] = kv_ref[:, A_KV_WIDTH:].astype(BF16)
        kc_scr[...] = kc_ref[0, 0].astype(BF16)
        vc_scr[...] = vc_ref[0, 0].astype(BF16)

    q = _rope(q_ref[...], cq_ref[...], sq_ref[...])
    start = pl.multiple_of(jnp.clip(n * ATTN_BLOCK - WINDOW, 0, t - span), ATTN_BLOCK)
    k_loc = k_scr[pl.ds(start, span), :]
    v_loc = v_scr[pl.ds(start, span), :]
    stacked = (A_GROUPS * ATTN_BLOCK, span)
    q_pos = n * ATTN_BLOCK + lax.broadcasted_iota(jnp.int32, stacked, 0) % ATTN_BLOCK
    k_pos = start + lax.broadcasted_iota(jnp.int32, stacked, 1)
    valid = jnp.logical_and(q_pos - k_pos <= WINDOW, k_pos - q_pos <= WINDOW)
    _gqa_attention(q, [(k_loc, v_loc, valid), (kc_scr[...], vc_scr[...], None)], sink_ref, o_ref)


def _attn_latent(proj, sink, cache_k, cache_v, layer, cos, sin, m_prompt, n_seq, t):
    kv_w = 2 * A_KV_WIDTH
    nb = t // ATTN_BLOCK
    past = cache_k.shape[2]
    row0 = m_prompt // ATTN_BLOCK
    seq0 = m_prompt // t
    return pl.pallas_call(
        functools.partial(_attn_lat_kernel, t=t),
        grid=(n_seq, nb),
        in_specs=[
            pl.BlockSpec(memory_space=pltpu.SMEM),
            pl.BlockSpec((ATTN_BLOCK, A_WIDTH), lambda b, n: (row0 + b * nb + n, CT_Q)),
            pl.BlockSpec((t, kv_w), lambda b, n: (seq0 + b, CT_MISC * COL_TILE // kv_w)),
            pl.BlockSpec((ATTN_BLOCK, A_WIDTH), lambda b, n: (n, 0)),
            pl.BlockSpec((ATTN_BLOCK, A_WIDTH), lambda b, n: (n, 0)),
            pl.BlockSpec((t, A_KV_WIDTH), lambda b, n: (0, 0)),
            pl.BlockSpec((t, A_KV_WIDTH), lambda b, n: (0, 0)),
            pl.BlockSpec((1, 1, past, A_KV_WIDTH), lambda b, n: (b, layer, 0, 0)),
            pl.BlockSpec((1, 1, past, A_KV_WIDTH), lambda b, n: (b, layer, 0, 0)),
        ],
        out_specs=pl.BlockSpec((ATTN_BLOCK, A_WIDTH), lambda b, n: (b * nb + n, 0)),
        out_shape=jax.ShapeDtypeStruct((n_seq * t, A_WIDTH), BF16),
        scratch_shapes=[
            pltpu.VMEM((t, A_KV_WIDTH), BF16),
            pltpu.VMEM((t, A_KV_WIDTH), BF16),
            pltpu.VMEM((past, A_KV_WIDTH), BF16),
            pltpu.VMEM((past, A_KV_WIDTH), BF16),
        ],
        compiler_params=_params(("parallel", "arbitrary")),
        name="attn_latent",
    )(sink, proj, proj, cos, sin, cos, sin, cache_k, cache_v)


def _rope_tables(t):
    pos = jnp.arange(t)
    half = HEAD_DIM // 2
    freqs = ROPE_BASE ** (-jnp.arange(0, half, 2, dtype=F32) / half)

    def tab(p):
        ang = p.astype(F32)[:, None] * freqs[None, :]
        c, s = jnp.cos(ang), jnp.sin(ang)
        return jnp.concatenate([c, c], -1), jnp.concatenate([-s, s], -1)

    c_row, s_row = tab(pos // GRID_W)
    c_col, s_col = tab(pos % GRID_W)
    cos = jnp.concatenate([c_row, c_col], -1)
    sin = jnp.concatenate([s_row, s_col], -1)
    return jnp.tile(cos, (1, A_HEADS)), jnp.tile(sin, (1, A_HEADS))


def _sgu_kernel(u_ref, v_ref, ng_ref, w_ref, b_ref, o_ref):
    for ch in range(SGU_TM // SGU_CHUNK):
        rs = slice(ch * SGU_CHUNK, (ch + 1) * SGU_CHUNK)
        for g in range(SGU_GROUPS):
            cs = slice(g * LANES, (g + 1) * LANES)
            vn = _layer_norm(_gelu(v_ref[rs, cs])) * ng_ref[:, cs]
            mixed = _dot(w_ref[g], vn.astype(BF16)) + b_ref[g]
            o_ref[rs, cs] = (_gelu(u_ref[rs, cs]) * mixed).astype(BF16)


def _sgu(proj, norm_g, w_s, bias):
    m = proj.shape[0]
    return pl.pallas_call(
        _sgu_kernel,
        grid=(m // SGU_TM,),
        in_specs=[
            pl.BlockSpec((SGU_TM, SGU_WIDTH), lambda i: (i, CT_SGU_U)),
            pl.BlockSpec((SGU_TM, SGU_WIDTH), lambda i: (i, CT_SGU_V)),
            pl.BlockSpec((1, SGU_WIDTH), lambda i: (0, 0)),
            pl.BlockSpec((SGU_GROUPS, SGU_CHUNK, SGU_CHUNK), lambda i: (0, 0, 0)),
            pl.BlockSpec((SGU_GROUPS, SGU_CHUNK, LANES), lambda i: (0, 0, 0)),
        ],
        out_specs=pl.BlockSpec((SGU_TM, SGU_WIDTH), lambda i: (i, 0)),
        out_shape=jax.ShapeDtypeStruct((m, SGU_WIDTH), BF16),
        compiler_params=_params(("parallel",)),
        name="sgu",
    )(proj, proj, norm_g, w_s, bias)


def _head_sums(x, ones_bd):
    return _dot(x, ones_bd, precision=HIGHEST)


def _rwkv_pre_kernel(x_ref, xp_ref, xn_ref, lo_ref, lop_ref, lon_ref, mu_ref, mul_ref, kk_ref, ka_ref,
                     rk_ref, w0_ref, a0_ref, w2_ref, a2_ref, g2_ref, ones_ref,
                     r_ref, v_ref, kkn_ref, gate_ref, bonus_ref,
                     lw0_ref, b0_ref, k0_ref, lw1_ref, b1_ref, k1_ref, *, n_prompt_tiles, tiles_per_seq):
    i = pl.program_id(0)
    is_prompt = i < n_prompt_tiles
    j = lax.rem(jnp.maximum(i - n_prompt_tiles, 0), tiles_per_seq)
    first = jnp.logical_or(is_prompt, j == 0)
    last = jnp.logical_or(is_prompt, j == tiles_per_seq - 1)

    def shift(x, xp, xn, mu):
        tm = x.shape[0]
        row = lax.broadcasted_iota(jnp.int32, x.shape, 0)
        prev_row = jnp.where(first, 0.0, xp[7:8, :])
        next_row = jnp.where(last, 0.0, xn[0:1, :])
        prev = jnp.where(row == 0, prev_row, pltpu.roll(x, 1, 0))
        nxt = jnp.where(row == tm - 1, next_row, pltpu.roll(x, tm - 1, 0))
        return x + mu[0:1, :] * (prev - x) + mu[1:2, :] * (nxt - x)

    xs = shift(x_ref[...], xp_ref[...], xn_ref[...], mu_ref[...])
    lo = shift(lo_ref[...], lop_ref[...], lon_ref[...], mul_ref[...])
    r = xs[:, :C_WIDTH]
    k = xs[:, C_WIDTH:2 * C_WIDTH]
    v = xs[:, 2 * C_WIDTH:]
    tw = jnp.tanh(lo[:, :W_LORA]).astype(BF16)
    al = lo[:, W_LORA:W_LORA + A_LORA].astype(BF16)
    gl = _sigmoid(lo[:, W_LORA + A_LORA:]).astype(BF16)
    ones_bd = ones_ref[...]

    r_ref[...] = r
    v_ref[...] = v
    gate_ref[...] = _dot(gl, g2_ref[...])
    kk0 = k * kk_ref[...]
    for p in range(HEAD_PAIRS):
        cs = slice(p * LANES, (p + 1) * LANES)
        ss = _head_sums(kk0[:, cs] * kk0[:, cs], ones_bd)
        kkn_ref[:, cs] = kk0[:, cs] * lax.rsqrt(jnp.maximum(ss, 1e-24))
    kkn = kkn_ref[...]

    bonus = None
    for d, (lw_ref, b_ref, kd_ref) in enumerate(((lw0_ref, b0_ref, k0_ref), (lw1_ref, b1_ref, k1_ref))):
        z = w0_ref[d:d + 1, :] + _dot(tw, w2_ref[d])
        softplus_neg = jnp.maximum(-z, 0.0) + jnp.log(1.0 + jnp.exp(-jnp.abs(z)))
        lw_ref[...] = -jnp.exp(-softplus_neg - 0.5)
        a = _sigmoid(a0_ref[d:d + 1, :] + _dot(al, a2_ref[d]))
        kd = k * (1.0 + (a - 1.0) * ka_ref[...])
        kd_ref[...] = kd
        b_ref[...] = kkn * a
        rkr = r * kd * rk_ref[...]
        parts = [_head_sums(rkr[:, p * LANES:(p + 1) * LANES], ones_bd) for p in range(HEAD_PAIRS)]
        term = jnp.concatenate(parts, axis=1) * v
        bonus = term if bonus is None else bonus + term
    bonus_ref[...] = bonus


def _rwkv_pre(proj, lp, ones_bd, m_prompt, t_sample):
    m = proj.shape[0]
    tm = PRE_TM
    rkv_w = 3 * C_WIDTH
    lo_blk = (CT_MISC * COL_TILE + 2 * A_KV_WIDTH) // LORA_COLS
    n8 = m // 8
    prev8 = lambda i: jnp.maximum(i * (tm // 8) - 1, 0)
    next8 = lambda i: jnp.minimum((i + 1) * (tm // 8), n8 - 1)
    full = lambda shape: pl.BlockSpec(shape, lambda i: (0,) * len(shape))
    out_spec = pl.BlockSpec((tm, C_WIDTH), lambda i: (i, 0))
    out_shape = jax.ShapeDtypeStruct((m, C_WIDTH), F32)
    kern = functools.partial(_rwkv_pre_kernel, n_prompt_tiles=m_prompt // tm, tiles_per_seq=t_sample // tm)
    return pl.pallas_call(
        kern,
        grid=(m // tm,),
        in_specs=[
            pl.BlockSpec((tm, rkv_w), lambda i: (i, CT_RKV * COL_TILE // rkv_w)),
            pl.BlockSpec((8, rkv_w), lambda i: (prev8(i), CT_RKV * COL_TILE // rkv_w)),
            pl.BlockSpec((8, rkv_w), lambda i: (next8(i), CT_RKV * COL_TILE // rkv_w)),
            pl.BlockSpec((tm, LORA_COLS), lambda i: (i, lo_blk)),
            pl.BlockSpec((8, LORA_COLS), lambda i: (prev8(i), lo_blk)),
            pl.BlockSpec((8, LORA_COLS), lambda i: (next8(i), lo_blk)),
            full((2, rkv_w)), full((2, LORA_COLS)), full((1, C_WIDTH)), full((1, C_WIDTH)),
            full((1, C_WIDTH)), full((2, C_WIDTH)), full((2, C_WIDTH)),
            full((2, W_LORA, C_WIDTH)), full((2, A_LORA, C_WIDTH)), full((G_LORA, C_WIDTH)),
            full((LANES, LANES)),
        ],
        out_specs=[out_spec] * 11,
        out_shape=[out_shape] * 11,
        compiler_params=_params(("parallel",)),
        name="rwkv_pre",
    )(proj, proj, proj, proj, proj, proj, lp["mu_rkv"], lp["mu_lora"], lp["k_k"], lp["k_a"], lp["r_k"],
      lp["w0"], lp["a0"], lp["w2"], lp["a2"], lp["g2"], ones_bd)


def _expand_heads(x, lane_lo):
    return jnp.concatenate([jnp.where(lane_lo, x, 0.0), jnp.where(lane_lo, 0.0, x)], axis=0).astype(BF16)


def _scan_masks(reverse):
    c = SCAN_CHUNK
    ri = lax.broadcasted_iota(jnp.int32, (2 * c, 2 * c), 0)
    ci = lax.broadcasted_iota(jnp.int32, (2 * c, 2 * c), 1)
    same = (ri // c) == (ci // c)
    tr, tc = ri % c, ci % c
    strict = jnp.logical_and(same, (tc > tr) if reverse else (tc < tr))
    incl = jnp.logical_and(same, (tc >= tr) if reverse else (tc <= tr))
    levels = []
    s = 1
    while s < c:
        late, early = (ci, ri) if reverse else (ri, ci)
        blk = jnp.logical_and((ri // (2 * s)) == (ci // (2 * s)),
                              jnp.logical_and(late % (2 * s) >= s, early % (2 * s) < s))
        levels.append(blk)
        s *= 2
    eye = ri == ci
    return strict, incl, levels, eye


def _scan_chunks(chains):
    c = SCAN_CHUNK
    n = 2 * c
    lane_lo = lax.broadcasted_iota(jnp.int32, (c, LANES), 1) < HEAD_DIM
    n_levels = len(chains[0][8][2])

    prep = []
    for r, v, kk, lw, b, k, cum, state, masks, reverse in chains:
        cum_end = cum[0:1, :] if reverse else cum[c - 1:c, :]
        inv = jnp.exp(-cum)
        tail = jnp.exp(cum_end - cum)
        ar = jnp.concatenate([_expand_heads(kk * jnp.exp(cum - lw), lane_lo),
                              _expand_heads(r * jnp.exp(cum), lane_lo)], axis=0)
        bk = jnp.concatenate([_expand_heads(b * inv, lane_lo), _expand_heads(k * inv, lane_lo)], axis=0)
        bk_tail = jnp.concatenate([_expand_heads(b * tail, lane_lo), _expand_heads(k * tail, lane_lo)], axis=0)
        prep.append((ar, bk, bk_tail, _expand_heads(v, lane_lo), jnp.exp(cum_end)))

    gs = [_dot_nt(ar, bk) for ar, bk, _, _, _ in prep]
    xs = [_dot_nt(p[0], ch[7].astype(BF16)) for p, ch in zip(prep, chains)]
    rhs = [x[:n] + _dot(jnp.where(ch[8][0], g[:n, n:], 0.0).astype(BF16), p[3])
           for x, g, p, ch in zip(xs, gs, prep, chains)]
    t_inv = [jnp.where(ch[8][2][0], -g[:n, :n], jnp.where(ch[8][3], 1.0, 0.0)) for g, ch in zip(gs, chains)]
    for lvl in range(1, n_levels):
        t_b = [t.astype(BF16) for t in t_inv]
        half = [_dot(tb, jnp.where(ch[8][2][lvl], g[:n, :n], 0.0).astype(BF16)).astype(BF16)
                for tb, g, ch in zip(t_b, gs, chains)]
        t_inv = [t - _dot(h, tb) for t, h, tb in zip(t_inv, half, t_b)]
    us = [-_dot(t.astype(BF16), z.astype(BF16)) for t, z in zip(t_inv, rhs)]
    uvs = [jnp.concatenate([u.astype(BF16), p[3]], axis=0) for u, p in zip(us, prep)]
    out = []
    for x, g, uv, p, ch in zip(xs, gs, uvs, prep, chains):
        incl = ch[8][1]
        d_r = jnp.concatenate([jnp.where(incl, g[n:, :n], 0.0), jnp.where(incl, g[n:, n:], 0.0)], axis=1)
        ye = x[n:] + _dot(d_r.astype(BF16), uv)
        out.append(ye[:c] + ye[c:])
    new_states = [ch[7] * p[4] + _dot_tn(uv, p[2]) for uv, p, ch in zip(uvs, prep, chains)]
    return list(zip(out, new_states))


def _scan_kernel(*refs, zero_init):
    fwd, bwd, (tri_f, tri_b) = refs[0:6], refs[6:12], refs[12:14]
    if zero_init:
        yf_ref, yb_ref, sfin_ref, s_ref = refs[14:18]
    else:
        s0_ref, yf_ref, yb_ref, s_ref = refs[14:18]
    head_block = lambda h: (slice(h * HEAD_DIM, (h + 1) * HEAD_DIM),) * 2

    @pl.when(pl.program_id(1) == 0)
    def _():
        s_ref[...] = jnp.zeros(s_ref.shape, F32)
        if not zero_init:
            for d in range(2):
                for p in range(HEAD_PAIRS):
                    for h in range(2):
                        s_ref[(d, p) + head_block(h)] = s0_ref[0, d, 2 * p + h]

    chains, dests = [], []
    for d, (ins, tri, y_ref) in enumerate(((fwd, tri_f, yf_ref), (bwd, tri_b, yb_ref))):
        r_ref, v_ref, kk_ref, lw_ref, b_ref, k_ref = ins
        cum_all = _dot(tri[...], lw_ref[...], precision=HIGHEST)
        masks = _scan_masks(reverse=(d == 1))
        for p in range(HEAD_PAIRS):
            cs = slice(p * LANES, (p + 1) * LANES)
            chains.append((r_ref[:, cs], v_ref[:, cs], kk_ref[:, cs], lw_ref[:, cs], b_ref[:, cs], k_ref[:, cs],
                           cum_all[:, cs], s_ref[d, p], masks, d == 1))
            dests.append((y_ref, d, p, cs))
    for (y_ref, d, p, cs), (y, new_state) in zip(dests, _scan_chunks(chains)):
        y_ref[:, cs] = y
        s_ref[d, p] = new_state

    if zero_init:
        @pl.when(pl.program_id(1) == pl.num_programs(1) - 1)
        def _():
            for d in range(2):
                for p in range(HEAD_PAIRS):
                    for h in range(2):
                        sfin_ref[0, d, 2 * p + h] = s_ref[(d, p) + head_block(h)]


def _rwkv_scan(pre, states, layer, row0, n_seq, t):
    r, v, kk, _, _, lw0, b0, k0, lw1, b1, k1 = pre
    c = SCAN_CHUNK
    nc = t // c
    blk0 = row0 // c
    fspec = pl.BlockSpec((c, C_WIDTH), lambda s, j: (blk0 + s * nc + j, 0))
    bspec = pl.BlockSpec((c, C_WIDTH), lambda s, j: (blk0 + s * nc + nc - 1 - j, 0))
    sblock = (1, 2, C_HEADS, HEAD_DIM, HEAD_DIM)
    tspec = pl.BlockSpec((c, c), lambda s, j: (0, 0))
    tri_f = jnp.tril(jnp.ones((c, c), F32))
    args = [r, v, kk, lw0, b0, k0, r, v, kk, lw1, b1, k1, tri_f, tri_f.T]
    in_specs = [fspec] * 6 + [bspec] * 6 + [tspec, tspec]
    out_specs = [pl.BlockSpec((c, C_WIDTH), lambda s, j: (s * nc + j, 0)),
                 pl.BlockSpec((c, C_WIDTH), lambda s, j: (s * nc + nc - 1 - j, 0))]
    out_shape = [jax.ShapeDtypeStruct((n_seq * t, C_WIDTH), F32)] * 2
    if states is None:
        out_specs.append(pl.BlockSpec(sblock, lambda s, j: (s, 0, 0, 0, 0)))
        out_shape.append(jax.ShapeDtypeStruct((n_seq,) + sblock[1:], F32))
    else:
        args.append(states)
        in_specs.append(pl.BlockSpec(sblock, lambda s, j: (s, layer, 0, 0, 0)))
    return pl.pallas_call(
        functools.partial(_scan_kernel, zero_init=states is None),
        grid=(n_seq, nc),
        in_specs=in_specs,
        out_specs=out_specs,
        out_shape=out_shape,
        scratch_shapes=[pltpu.VMEM((2, HEAD_PAIRS, LANES, LANES), F32)],
        compiler_params=_params(("parallel", "arbitrary")),
        name="rwkv_scan",
    )(*args)


def _rwkv_post_kernel(yfp_ref, ybp_ref, yfs_ref, ybs_ref, bonus_ref, gate_ref, g_ref, b_ref, ones_ref, o_ref,
                      y_scr, *, n_prompt_tiles):
    i = pl.program_id(0)

    @pl.when(i < n_prompt_tiles)
    def _():
        y_scr[...] = yfp_ref[...] + ybp_ref[...]

    @pl.when(i >= n_prompt_tiles)
    def _():
        y_scr[...] = yfs_ref[...] + ybs_ref[...]

    ones_bd = ones_ref[...]
    for p in range(HEAD_PAIRS):
        cs = slice(p * LANES, (p + 1) * LANES)
        y = y_scr[:, cs]
        mu = _head_sums(y, ones_bd) * (1.0 / HEAD_DIM)
        yc = y - mu
        var = _head_sums(yc * yc, ones_bd) * (1.0 / HEAD_DIM)
        yn = yc * lax.rsqrt(var + GN_EPS) * g_ref[:, cs] + b_ref[:, cs]
        o_ref[:, cs] = ((yn + bonus_ref[:, cs]) * gate_ref[:, cs]).astype(BF16)


def _rwkv_post(y_prompt, y_sample, bonus, gate, ln_g, ln_b, ones_bd):
    m = bonus.shape[0]
    n_p = y_prompt[0].shape[0] // POST_TM
    spec = pl.BlockSpec((POST_TM, C_WIDTH), lambda i: (i, 0))
    pspec = pl.BlockSpec((POST_TM, C_WIDTH), lambda i: (jnp.minimum(i, n_p - 1), 0))
    sspec = pl.BlockSpec((POST_TM, C_WIDTH), lambda i: (jnp.maximum(i - n_p, 0), 0))
    vec = pl.BlockSpec((1, C_WIDTH), lambda i: (0, 0))
    return pl.pallas_call(
        functools.partial(_rwkv_post_kernel, n_prompt_tiles=n_p),
        grid=(m // POST_TM,),
        in_specs=[pspec, pspec, sspec, sspec, spec, spec, vec, vec, pl.BlockSpec((LANES, LANES), lambda i: (0, 0))],
        out_specs=spec,
        out_shape=jax.ShapeDtypeStruct((m, C_WIDTH), BF16),
        scratch_shapes=[pltpu.VMEM((POST_TM, C_WIDTH), F32)],
        compiler_params=_params(("parallel",)),
        name="rwkv_post",
    )(*y_prompt, *y_sample, bonus, gate, ln_g, ln_b, ones_bd)


def _merge_kernel(x_ref, attn_p_ref, attn_s_ref, sgu_ref, rw_ref, ga_ref, gb_ref, gc_ref, bg_ref, g1_ref,
                  wa_ref, wb_ref, wc_ref, wo_ref, lg_ref, lb_ref, o_ref, attn_scr, *, n_prompt_tiles):
    i = pl.program_id(0)

    @pl.when(i < n_prompt_tiles)
    def _():
        attn_scr[...] = attn_p_ref[...]

    @pl.when(i >= n_prompt_tiles)
    def _():
        attn_scr[...] = attn_s_ref[...]

    merged = None
    for br, (a_ref, w_ref, g_ref) in enumerate(((attn_scr, wa_ref, ga_ref), (sgu_ref, wb_ref, gb_ref),
                                                 (rw_ref, wc_ref, gc_ref))):
        gate = _sigmoid(g_ref[...] + bg_ref[:, br * D_MODEL:(br + 1) * D_MODEL])
        term = gate * _dot(a_ref[...], w_ref[...])
        merged = term if merged is None else merged + term
    out = _dot(merged.astype(BF16), wo_ref[...])
    y = _layer_norm(ALPHA * x_ref[...] + g1_ref[0] * out)
    o_ref[...] = y * lg_ref[...] + lb_ref[...]


def _merge(x, attn_p, attn_s, sgu, rw, proj, mod3, lp, m_prompt, t_sample):
    m = x.shape[0]
    tm = MERGE_TM
    n_p = m_prompt // tm
    req = functools.partial(_request_of_tile, tm=tm, m_prompt=m_prompt, t_sample=t_sample)
    row = lambda w: pl.BlockSpec((tm, w), lambda i: (i, 0))
    const = lambda shape: pl.BlockSpec(shape, lambda i: (0,) * len(shape), pipeline_mode=pl.Buffered(1))
    gate = lambda br: pl.BlockSpec((tm, D_MODEL), lambda i: (i, CT_GATE * COL_TILE // D_MODEL + br))
    return pl.pallas_call(
        functools.partial(_merge_kernel, n_prompt_tiles=n_p),
        grid=(m // tm,),
        in_specs=[
            row(D_MODEL),
            pl.BlockSpec((tm, A_WIDTH), lambda i: (jnp.minimum(i, n_p - 1), 0)),
            pl.BlockSpec((tm, A_WIDTH), lambda i: (jnp.maximum(i - n_p, 0), 0)),
            row(SGU_WIDTH), row(C_WIDTH), gate(0), gate(1), gate(2),
            const((1, N_BRANCH * D_MODEL)),
            pl.BlockSpec((1, 1, D_MODEL), lambda i: (req(i) * 6 + 2, 0, 0)),
            const((A_WIDTH, D_MODEL)), const((SGU_WIDTH, D_MODEL)), const((C_WIDTH, D_MODEL)),
            const((D_MODEL, D_MODEL)), const((1, D_MODEL)), const((1, D_MODEL)),
        ],
        out_specs=row(D_MODEL),
        out_shape=jax.ShapeDtypeStruct((m, D_MODEL), F32),
        scratch_shapes=[pltpu.VMEM((tm, A_WIDTH), BF16)],
        compiler_params=_params(("parallel",)),
        name="merge",
    )(x, attn_p, attn_s, sgu, rw, proj, proj, proj, lp["b_gate"], mod3, lp["w_branch_a"], lp["w_branch_b"],
      lp["w_branch_c"], lp["w_out"], lp["ln1_g"], lp["ln1_b"])


def _ffn_kernel(x_ref, sh_ref, sc_ref, g2_ref, wg_ref, wu_ref, wo_ref, lg_ref, lb_ref, o_ref, h_scr, acc_scr):
    f = pl.program_id(1)

    @pl.when(f == 0)
    def _():
        h_scr[...] = (_layer_norm(x_ref[...]) * (1.0 + sc_ref[0]) + sh_ref[0]).astype(BF16)
        acc_scr[...] = jnp.zeros(acc_scr.shape, F32)

    h = h_scr[...]
    g = _dot(h, wg_ref[...])
    u = _dot(h, wu_ref[...])
    acc_scr[...] += _dot((g * _sigmoid(g) * u).astype(BF16), wo_ref[...])

    @pl.when(f == pl.num_programs(1) - 1)
    def _():
        y = _layer_norm(ALPHA * x_ref[...] + g2_ref[0] * acc_scr[...])
        o_ref[...] = y * lg_ref[...] + lb_ref[...]


def _ffn_dense(x, mod3, w_in, w_out, ln_g, ln_b, m_prompt, t_sample):
    m = x.shape[0]
    tm, tf = FFN_TM, FFN_TF
    nf = D_FF // tf
    req = functools.partial(_request_of_tile, tm=tm, m_prompt=m_prompt, t_sample=t_sample)
    modspec = lambda k: pl.BlockSpec((1, 1, D_MODEL), lambda i, f: (req(i) * 6 + k, 0, 0))
    vec = pl.BlockSpec((1, D_MODEL), lambda i, f: (0, 0))
    return pl.pallas_call(
        _ffn_kernel,
        grid=(m // tm, nf),
        in_specs=[
            pl.BlockSpec((tm, D_MODEL), lambda i, f: (i, 0)),
            modspec(3), modspec(4), modspec(5),
            pl.BlockSpec((D_MODEL, tf), lambda i, f: (0, f)),
            pl.BlockSpec((D_MODEL, tf), lambda i, f: (0, nf + f)),
            pl.BlockSpec((tf, D_MODEL), lambda i, f: (f, 0)),
            vec, vec,
        ],
        out_specs=pl.BlockSpec((tm, D_MODEL), lambda i, f: (i, 0)),
        out_shape=jax.ShapeDtypeStruct((m, D_MODEL), F32),
        scratch_shapes=[pltpu.VMEM((tm, D_MODEL), BF16), pltpu.VMEM((tm, D_MODEL), F32)],
        compiler_params=_params(("parallel", "arbitrary")),
        name="ffn_dense",
    )(x, mod3, mod3, mod3, w_in, w_in, w_out, ln_g, ln_b)


def _route_kernel(x_ref, sh_ref, sc_ref, wr_ref, h_ref, idx_ref, gate_ref):
    h = _layer_norm(x_ref[...]) * (1.0 + sc_ref[0]) + sh_ref[0]
    h_ref[...] = h
    logits = _dot(h, wr_ref[...], precision=HIGHEST)
    lane = lax.broadcasted_iota(jnp.int32, logits.shape, 1)
    logits = jnp.where(lane < N_EXPERTS, logits, -jnp.inf)
    lane_f = lane.astype(F32)
    m1 = jnp.max(logits, axis=-1, keepdims=True)
    i1 = jnp.min(jnp.where(logits == m1, lane_f, float(LANES)), axis=-1, keepdims=True)
    rest = jnp.where(lane_f == i1, -jnp.inf, logits)
    m2 = jnp.max(rest, axis=-1, keepdims=True)
    i2 = jnp.min(jnp.where(rest == m2, lane_f, float(LANES)), axis=-1, keepdims=True)
    e = jnp.exp(m2 - m1)
    g1 = 1.0 / (1.0 + e)
    idx_ref[...] = jnp.where(lane == 0, i1, jnp.where(lane == 1, i2, 0.0)).astype(jnp.int32)
    gate_ref[...] = jnp.where(lane == 0, g1, jnp.where(lane == 1, e * g1, 0.0))


def _route(x, mod3, router_p, m_prompt, t_sample):
    m = x.shape[0]
    tm = ROUTE_TM
    req = functools.partial(_request_of_tile, tm=tm, m_prompt=m_prompt, t_sample=t_sample)
    modspec = lambda k: pl.BlockSpec((1, 1, D_MODEL), lambda i: (req(i) * 6 + k, 0, 0))
    return pl.pallas_call(
        _route_kernel,
        grid=(m // tm,),
        in_specs=[pl.BlockSpec((tm, D_MODEL), lambda i: (i, 0)), modspec(3), modspec(4),
                  pl.BlockSpec((D_MODEL, LANES), lambda i: (0, 0))],
        out_specs=[pl.BlockSpec((tm, D_MODEL), lambda i: (i, 0)), pl.BlockSpec((tm, LANES), lambda i: (i, 0)),
                   pl.BlockSpec((tm, LANES), lambda i: (i, 0))],
        out_shape=[jax.ShapeDtypeStruct((m, D_MODEL), F32), jax.ShapeDtypeStruct((m, LANES), jnp.int32),
                   jax.ShapeDtypeStruct((m, LANES), F32)],
        compiler_params=_params(("parallel",)),
        name="route",
    )(x, mod3, mod3, router_p)


def _row_copy(src_hbm, row, dst, dst_row, sem):
    return pltpu.make_async_copy(src_hbm.at[pl.ds(row, 1)], dst.at[pl.ds(dst_row, 1)], sem)


def _moe_kernel(tile_e_ref, tile_valid_ref, row_tok_ref, h_hbm, wg_ref, wu_ref, wo_ref, o_ref, x_scr, sem):
    i = pl.program_id(0)
    f = pl.program_id(1)
    valid = tile_valid_ref[i]

    @pl.when(f == 0)
    def _():
        o_ref[...] = jnp.zeros(o_ref.shape, F32)

        @pl.when(valid > 0)
        def _():
            def start(r2, carry):
                for prio in range(2):
                    r = 2 * r2 + prio
                    _row_copy(h_hbm, row_tok_ref[0, 0, r], x_scr, r, sem).start(priority=prio)
                return carry

            def wait(r, carry):
                _row_copy(h_hbm, 0, x_scr, r, sem).wait()
                return carry

            lax.fori_loop(0, MOE_TM // 2, start, 0)
            lax.fori_loop(0, MOE_TM, wait, 0)

    @pl.when(valid > 0)
    def _():
        n_sub = lax.shift_right_logical(valid + (MOE_SUB - 1), MOE_SUB.bit_length() - 1)
        for nb in range(1, MOE_TM // MOE_SUB + 1):
            @pl.when(n_sub == nb)
            def _():
                rows = nb * MOE_SUB
                x = x_scr[:rows, :].astype(BF16)
                g = _dot(x, wg_ref[0].astype(BF16))
                u = _dot(x, wu_ref[0].astype(BF16))
                o_ref[:rows, :] += _dot((g * _sigmoid(g) * u).astype(BF16), wo_ref[0].astype(BF16))


def _moe_experts(h, row_tok, tile_e, tile_valid, w_in, w_out):
    n_tiles = tile_e.shape[0]
    nf = D_FF_EXPERT // MOE_TF
    fidx = lambda i, f, tv: jnp.where(tv[i] > 0, f, nf - 1)
    grid_spec = pltpu.PrefetchScalarGridSpec(
        num_scalar_prefetch=2,
        grid=(n_tiles, nf),
        in_specs=[
            pl.BlockSpec((1, 1, MOE_TM), lambda i, f, te, tv: (i, 0, 0), memory_space=pltpu.SMEM),
            pl.BlockSpec(memory_space=pl.ANY),
            pl.BlockSpec((1, D_MODEL, MOE_TF), lambda i, f, te, tv: (te[i], 0, fidx(i, f, tv))),
            pl.BlockSpec((1, D_MODEL, MOE_TF), lambda i, f, te, tv: (te[i], 0, nf + fidx(i, f, tv))),
            pl.BlockSpec((1, MOE_TF, D_MODEL), lambda i, f, te, tv: (te[i], fidx(i, f, tv), 0)),
        ],
        out_specs=pl.BlockSpec((MOE_TM, D_MODEL), lambda i, f, te, tv: (i, 0), pipeline_mode=pl.Buffered(1)),
        scratch_shapes=[pltpu.VMEM((MOE_TM, D_MODEL), F32), pltpu.SemaphoreType.DMA(())],
    )
    return pl.pallas_call(
        _moe_kernel,
        grid_spec=grid_spec,
        out_shape=jax.ShapeDtypeStruct((n_tiles * MOE_TM, D_MODEL), F32),
        compiler_params=_params(("arbitrary", "arbitrary")),
        name="moe_experts",
    )(tile_e, tile_valid, row_tok.reshape(n_tiles, 1, MOE_TM), h, w_in, w_in, w_out)


def _combine_kernel(pos_ref, ys_hbm, x_ref, gate_ref, g2_ref, lg_ref, lb_ref, o_ref, buf, sem):
    tm = COMB_TM

    def start(t, carry):
        for k in range(TOP_K):
            _row_copy(ys_hbm, pos_ref[0, 0, t * TOP_K + k], buf.at[k], t, sem).start(priority=k)
        return carry

    def wait(t, carry):
        for k in range(TOP_K):
            _row_copy(ys_hbm, 0, buf.at[k], t, sem).wait()
        return carry

    lax.fori_loop(0, tm, start, 0)
    lax.fori_loop(0, tm, wait, 0)
    y = gate_ref[:, 0:1] * buf[0] + gate_ref[:, 1:2] * buf[1]
    z = _layer_norm(ALPHA * x_ref[...] + g2_ref[0] * y)
    o_ref[...] = z * lg_ref[...] + lb_ref[...]


def _moe_combine(x, ys, pos, gates, mod3, ln_g, ln_b, m_prompt, t_sample):
    m = x.shape[0]
    tm = COMB_TM
    req = functools.partial(_request_of_tile, tm=tm, m_prompt=m_prompt, t_sample=t_sample)
    return pl.pallas_call(
        _combine_kernel,
        grid=(m // tm,),
        in_specs=[
            pl.BlockSpec((1, 1, tm * TOP_K), lambda i: (i, 0, 0), memory_space=pltpu.SMEM),
            pl.BlockSpec(memory_space=pl.ANY),
            pl.BlockSpec((tm, D_MODEL), lambda i: (i, 0)),
            pl.BlockSpec((tm, LANES), lambda i: (i, 0)),
            pl.BlockSpec((1, 1, D_MODEL), lambda i: (req(i) * 6 + 5, 0, 0)),
            pl.BlockSpec((1, D_MODEL), lambda i: (0, 0)),
            pl.BlockSpec((1, D_MODEL), lambda i: (0, 0)),
        ],
        out_specs=pl.BlockSpec((tm, D_MODEL), lambda i: (i, 0)),
        out_shape=jax.ShapeDtypeStruct((m, D_MODEL), F32),
        scratch_shapes=[pltpu.VMEM((TOP_K, tm, D_MODEL), F32), pltpu.SemaphoreType.DMA(())],
        compiler_params=_params(("arbitrary",)),
        name="moe_combine",
    )(pos.reshape(m // tm, 1, tm * TOP_K), ys, x, gates, mod3, ln_g, ln_b)


def _moe_plan(idx):
    m = idx.shape[0]
    flat_e = idx.reshape(-1)
    n_assign = m * TOP_K
    onehot = (flat_e[:, None] == jnp.arange(N_EXPERTS, dtype=jnp.int32)[None, :]).astype(jnp.int32)
    csum = jnp.cumsum(onehot, axis=0)
    counts = csum[-1]
    rank = jnp.take_along_axis(csum, flat_e[:, None], axis=1)[:, 0] - 1
    padded = (counts + MOE_TM - 1) // MOE_TM * MOE_TM
    pad_end = jnp.cumsum(padded)
    pad_start = pad_end - padded
    pos = (pad_start[flat_e] + rank).astype(jnp.int32)
    n_tiles = n_assign // MOE_TM + N_EXPERTS
    row_tok = jnp.zeros((n_tiles * MOE_TM,), jnp.int32).at[pos].set(jnp.arange(n_assign, dtype=jnp.int32) // TOP_K)
    tile_start = jnp.arange(n_tiles, dtype=jnp.int32) * MOE_TM
    tile_e = jnp.sum((pad_end[None, :] <= tile_start[:, None]).astype(jnp.int32), axis=1)
    used = tile_e < N_EXPERTS
    last_e = jnp.max(jnp.where(counts > 0, jnp.arange(N_EXPERTS, dtype=jnp.int32), 0))
    tile_e = jnp.where(used, tile_e, last_e)
    seg_end = pad_start[tile_e] + counts[tile_e]
    tile_valid = jnp.where(used, jnp.clip(seg_end - tile_start, 0, MOE_TM), 0).astype(jnp.int32)
    return row_tok, tile_e, tile_valid, pos


def _reorder_w_in(w_in):
    o_q, o_k, o_v = 0, A_WIDTH, A_WIDTH + A_KV_WIDTH
    o_xb = o_v + A_KV_WIDTH
    o_xc = o_xb + 2 * SGU_WIDTH
    o_lora = o_xc + 3 * C_WIDTH
    o_gl = o_xc + C_COLS
    cols = lambda a, n: w_in[:, :, a:a + n]
    pad = jnp.zeros(w_in.shape[:2] + (COL_TILE - 2 * A_KV_WIDTH - LORA_COLS,), w_in.dtype)
    parts = [cols(o_gl, N_BRANCH * D_MODEL), cols(o_xb, 2 * SGU_WIDTH), cols(o_q, A_WIDTH),
             cols(o_xc, 3 * C_WIDTH), cols(o_k, 2 * A_KV_WIDTH), cols(o_lora, LORA_COLS), pad]
    return jnp.concatenate(parts, axis=-1).astype(BF16)


def kernel(x_prompt, x_sample, cache_k, cache_v, state_rwkv, c, c_ctx, w_mod, b_mod, w_in, b_gate, attn_sink, sgu_norm_g, sgu_w, sgu_b, rwkv_mu, rwkv_w0, rwkv_w2, rwkv_a0, rwkv_a2, rwkv_g2, rwkv_k_k, rwkv_k_a, rwkv_r_k, rwkv_ln_g, rwkv_ln_b, w_branch_a, w_branch_b, w_branch_c, w_out, ln1_g, ln1_b, ln2_g, ln2_b, ffn_w_in, ffn_w_out, moe_router, moe_w_in, moe_w_out):
    bp, tp, _ = x_prompt.shape
    bs, ts, _ = x_sample.shape
    past = cache_k.shape[2]
    mp, ms = bp * tp, bs * ts
    assert bs + 1 <= 16 and mp % IN_TM == 0 and ts % IN_TM == 0 and tp % PRE_TM == 0 and IN_TM % tp == 0

    x = jnp.concatenate([x_prompt.reshape(mp, D_MODEL), x_sample.reshape(ms, D_MODEL)], axis=0)
    cond = jnp.zeros((16, D_MODEL), F32).at[0].set(c_ctx).at[1:1 + bs].set(c)
    mod = _modulation(cond, w_mod, b_mod)

    w_in_p = _reorder_w_in(w_in)
    cos, sin = _rope_tables(ts)
    ones_bd = jnp.kron(jnp.eye(LANES // HEAD_DIM, dtype=F32), jnp.ones((HEAD_DIM, HEAD_DIM), F32))
    cache_k4 = cache_k.reshape(bs, DEPTH, past, A_KV_WIDTH)
    cache_v4 = cache_v.reshape(bs, DEPTH, past, A_KV_WIDTH)
    states4 = state_rwkv.astype(F32).reshape(bs, DEPTH * 2, C_HEADS, HEAD_DIM, HEAD_DIM)

    new_k, new_v, new_s = [], [], []
    for l in range(DEPTH):
        mod3 = mod[l].reshape(16 * 6, 1, D_MODEL)
        lp = {
            "mu_rkv": rwkv_mu[l][:, :3 * C_WIDTH], "mu_lora": rwkv_mu[l][:, 3 * C_WIDTH:],
            "k_k": rwkv_k_k[l].reshape(1, C_WIDTH), "k_a": rwkv_k_a[l].reshape(1, C_WIDTH),
            "r_k": rwkv_r_k[l].reshape(1, C_WIDTH), "w0": rwkv_w0[l], "a0": rwkv_a0[l],
            "w2": rwkv_w2[l].astype(BF16), "a2": rwkv_a2[l].astype(BF16), "g2": rwkv_g2[l].astype(BF16),
            "b_gate": b_gate[l].reshape(1, N_BRANCH * D_MODEL),
            "w_branch_a": w_branch_a[l].astype(BF16), "w_branch_b": w_branch_b[l].astype(BF16),
            "w_branch_c": w_branch_c[l].astype(BF16), "w_out": w_out[l].astype(BF16),
            "ln1_g": ln1_g[l].reshape(1, D_MODEL), "ln1_b": ln1_b[l].reshape(1, D_MODEL),
        }
        proj = _in_proj(x, mod3, w_in_p[l], mp, ts)

        attn_p = _attn_context(proj, attn_sink[l], bp, tp)
        attn_s = _attn_latent(proj, attn_sink[l], cache_k4, cache_v4, l, cos, sin, mp, bs, ts)
        kv_cols = proj[:mp, CT_MISC * COL_TILE:CT_MISC * COL_TILE + 2 * A_KV_WIDTH]
        new_k.append(kv_cols[:, :A_KV_WIDTH].reshape(bp, tp, A_KV_HEADS, HEAD_DIM))
        new_v.append(kv_cols[:, A_KV_WIDTH:].reshape(bp, tp, A_KV_HEADS, HEAD_DIM))

        sgu_bias = jnp.broadcast_to(sgu_b[l][:, :, None], (SGU_GROUPS, SGU_CHUNK, LANES))
        sgu = _sgu(proj, sgu_norm_g[l].reshape(1, SGU_WIDTH), sgu_w[l].astype(BF16), sgu_bias)

        pre = _rwkv_pre(proj, lp, ones_bd, mp, ts)
        yf_p, yb_p, s_fin = _rwkv_scan(pre, None, l, 0, bp, tp)
        yf_s, yb_s = _rwkv_scan(pre, states4, l, mp, bs, ts)
        rw = _rwkv_post((yf_p, yb_p), (yf_s, yb_s), pre[4], pre[3],
                        rwkv_ln_g[l].reshape(1, C_WIDTH), rwkv_ln_b[l].reshape(1, C_WIDTH), ones_bd)
        new_s.append(s_fin)

        x1 = _merge(x, attn_p, attn_s, sgu, rw, proj, mod3, lp, mp, ts)

        ln2g, ln2b = ln2_g[l].reshape(1, D_MODEL), ln2_b[l].reshape(1, D_MODEL)
        if l % 2 == 0:
            x = _ffn_dense(x1, mod3, ffn_w_in[l // 2].astype(BF16), ffn_w_out[l // 2].astype(BF16), ln2g, ln2b, mp, ts)
        else:
            router_p = jnp.zeros((D_MODEL, LANES), F32).at[:, :N_EXPERTS].set(moe_router[l // 2])
            h, idx, gates = _route(x1, mod3, router_p, mp, ts)
            row_tok, tile_e, tile_valid, pos = _moe_plan(idx[:, :TOP_K])
            ys = _moe_experts(h, row_tok, tile_e, tile_valid, moe_w_in[l // 2], moe_w_out[l // 2])
            x = _moe_combine(x1, ys, pos, gates, mod3, ln2g, ln2b, mp, ts)

    y_prompt = x[:mp].reshape(bp, tp, D_MODEL)
    y_sample = x[mp:].reshape(bs, ts, D_MODEL)
    return (y_prompt, y_sample, jnp.stack(new_k, axis=1), jnp.stack(new_v, axis=1), jnp.stack(new_s, axis=1))
```

```python
import functools

import jax
import jax.numpy as jnp
from jax import lax
from jax.experimental import pallas as pl
from jax.experimental.pallas import tpu as pltpu

F32 = jnp.float32
BF16 = jnp.bfloat16
HIGHEST = lax.Precision.HIGHEST

D_MODEL = 2048
DEPTH = 2
GRID_W = 64
A_HEADS = 16
A_KV_HEADS = 4
A_GROUPS = A_HEADS // A_KV_HEADS
HEAD_DIM = 64
A_WIDTH = A_HEADS * HEAD_DIM
A_KV_WIDTH = A_KV_HEADS * HEAD_DIM
ATTN_SCALE = HEAD_DIM ** -0.5
WINDOW = 128
ATTN_BLOCK = 128
ROPE_BASE = 10000.0
SGU_CHUNK = 128
SGU_GROUPS = 8
SGU_WIDTH = 1024
C_HEADS = 16
C_WIDTH = C_HEADS * HEAD_DIM
W_LORA = 64
A_LORA = 64
G_LORA = 128
LORA_COLS = W_LORA + A_LORA + G_LORA
C_COLS = 3 * C_WIDTH + LORA_COLS
N_BRANCH = 3
D_FF = 5632
N_EXPERTS = 8
TOP_K = 2
D_FF_EXPERT = 7168
ALPHA = (2 * DEPTH) ** 0.25
LN_EPS = 1e-6
GN_EPS = HEAD_DIM * 1e-5
NEG_INF = -1e30

LANES = 128
HEAD_PAIRS = C_HEADS * HEAD_DIM // LANES
SCAN_CHUNK = 64
VMEM_LIMIT = 56 * 2 ** 20

COL_TILE = 1024
N_COL_TILES = 13
PROJ_COLS = N_COL_TILES * COL_TILE
CT_GATE, CT_SGU_U, CT_SGU_V, CT_Q, CT_RKV, CT_MISC = 0, 6, 7, 8, 9, 12

IN_TM = 1024
MERGE_TM = 256
FFN_TM = 1024
FFN_TF = 512
PRE_TM = 256
POST_TM = 512
SGU_TM = 512
ROUTE_TM = 512
MOE_TM = 1024
MOE_SUB = 256
MOE_TF = 512
COMB_TM = 256


def _params(sem):
    return pltpu.CompilerParams(dimension_semantics=sem, vmem_limit_bytes=VMEM_LIMIT)


def _dot(a, b, precision=None):
    return jnp.dot(a, b, preferred_element_type=F32, precision=precision)


def _dot_nt(a, b):
    return lax.dot_general(a, b, (((1,), (1,)), ((), ())), preferred_element_type=F32)


def _dot_tn(a, b):
    return lax.dot_general(a, b, (((0,), (0,)), ((), ())), preferred_element_type=F32)


def _layer_norm(x):
    mu = jnp.mean(x, axis=-1, keepdims=True)
    xc = x - mu
    var = jnp.mean(xc * xc, axis=-1, keepdims=True)
    return xc * lax.rsqrt(var + LN_EPS)


def _gelu(x):
    return 0.5 * x * (1.0 + jnp.tanh(0.7978845608028654 * (x + 0.044715 * (x * x * x))))


def _sigmoid(x):
    return 1.0 / (1.0 + jnp.exp(-x))


def _request_of_tile(i, tm, m_prompt, t_sample):
    n_prompt_tiles = m_prompt // tm
    return jnp.where(i < n_prompt_tiles, 0, 1 + (i - n_prompt_tiles) // (t_sample // tm))


def _mod_kernel(c_ref, w_ref, b_ref, o_ref):
    c = c_ref[...]
    h = (c * _sigmoid(c)).astype(BF16)
    o_ref[0] = _dot(h, w_ref[0].astype(BF16)) + b_ref[0]


def _modulation(cond, w_mod, b_mod):
    tn = 1024
    n = w_mod.shape[-1]
    return pl.pallas_call(
        _mod_kernel,
        grid=(DEPTH, n // tn),
        in_specs=[
            pl.BlockSpec((16, D_MODEL), lambda l, j: (0, 0)),
            pl.BlockSpec((1, D_MODEL, tn), lambda l, j: (l, 0, j)),
            pl.BlockSpec((1, 1, tn), lambda l, j: (l, 0, j)),
        ],
        out_specs=pl.BlockSpec((1, 16, tn), lambda l, j: (l, 0, j)),
        out_shape=jax.ShapeDtypeStruct((DEPTH, 16, n), F32),
        compiler_params=_params(("parallel", "parallel")),
        name="modulation",
    )(cond, w_mod, b_mod.reshape(DEPTH, 1, n))


def _in_kernel(x_ref, sh_ref, sc_ref, w_ref, o_ref, h_scr):
    @pl.when(pl.program_id(1) == 0)
    def _():
        y = _layer_norm(x_ref[...])
        h_scr[...] = (y * (1.0 + sc_ref[0]) + sh_ref[0]).astype(BF16)

    o_ref[...] = _dot(h_scr[...], w_ref[...])


def _in_proj(x, mod3, w_in_p, m_prompt, t_sample):
    m = x.shape[0]
    req = functools.partial(_request_of_tile, tm=IN_TM, m_prompt=m_prompt, t_sample=t_sample)
    return pl.pallas_call(
        _in_kernel,
        grid=(m // IN_TM, N_COL_TILES),
        in_specs=[
            pl.BlockSpec((IN_TM, D_MODEL), lambda i, j: (i, 0)),
            pl.BlockSpec((1, 1, D_MODEL), lambda i, j: (req(i) * 6 + 0, 0, 0)),
            pl.BlockSpec((1, 1, D_MODEL), lambda i, j: (req(i) * 6 + 1, 0, 0)),
            pl.BlockSpec((D_MODEL, COL_TILE), lambda i, j: (0, j)),
        ],
        out_specs=pl.BlockSpec((IN_TM, COL_TILE), lambda i, j: (i, j)),
        out_shape=jax.ShapeDtypeStruct((m, PROJ_COLS), F32),
        scratch_shapes=[pltpu.VMEM((IN_TM, D_MODEL), BF16)],
        compiler_params=_params(("parallel", "arbitrary")),
        name="in_proj",
    )(x, mod3, mod3, w_in_p)


def _softmax_parts(scores, sink):
    m = sink
    for s in scores:
        m = jnp.maximum(m, jnp.max(s, axis=-1, keepdims=True))
    ps = [jnp.exp(s - m) for s in scores]
    den = jnp.exp(sink - m)
    for p in ps:
        den = den + jnp.sum(p, axis=-1, keepdims=True)
    inv = 1.0 / den
    return [(p * inv).astype(BF16) for p in ps]


def _gqa_attention(q, key_sets, sink_ref, o_ref):
    rows = q.shape[0]
    col = lambda kvh: slice(kvh * HEAD_DIM, (kvh + 1) * HEAD_DIM)
    heads = lambda kvh: range(kvh * A_GROUPS, (kvh + 1) * A_GROUPS)
    masks = [m for _, _, m in key_sets]
    qs = [jnp.concatenate([q[:, h * HEAD_DIM:(h + 1) * HEAD_DIM] for h in heads(kvh)], axis=0).astype(BF16)
          for kvh in range(A_KV_HEADS)]
    scores = []
    for kvh in range(A_KV_HEADS):
        per_set = []
        for (k, _, _), m in zip(key_sets, masks):
            s = _dot_nt(qs[kvh], k[:, col(kvh)]) * ATTN_SCALE
            per_set.append(s if m is None else jnp.where(m, s, NEG_INF))
        scores.append(per_set)
    probs = []
    for kvh in range(A_KV_HEADS):
        sink = jnp.concatenate([jnp.full((rows, 1), sink_ref[h], F32) for h in heads(kvh)], axis=0)
        probs.append(_softmax_parts(scores[kvh], sink))
    for kvh in range(A_KV_HEADS):
        o = None
        for (_, v, _), p in zip(key_sets, probs[kvh]):
            term = _dot(p, v[:, col(kvh)])
            o = term if o is None else o + term
        for g, h in enumerate(heads(kvh)):
            o_ref[:, h * HEAD_DIM:(h + 1) * HEAD_DIM] = o[g * rows:(g + 1) * rows].astype(BF16)


def _attn_ctx_kernel(sink_ref, q_ref, kv_ref, o_ref):
    k = kv_ref[:, :A_KV_WIDTH].astype(BF16)
    v = kv_ref[:, A_KV_WIDTH:].astype(BF16)
    _gqa_attention(q_ref[...], [(k, v, None)], sink_ref, o_ref)


def _attn_context(proj, sink, n_seq, t):
    kv_w = 2 * A_KV_WIDTH
    return pl.pallas_call(
        _attn_ctx_kernel,
        grid=(n_seq,),
        in_specs=[
            pl.BlockSpec(memory_space=pltpu.SMEM),
            pl.BlockSpec((t, A_WIDTH), lambda b: (b, CT_Q)),
            pl.BlockSpec((t, kv_w), lambda b: (b, CT_MISC * COL_TILE // kv_w)),
        ],
        out_specs=pl.BlockSpec((t, A_WIDTH), lambda b: (b, 0)),
        out_shape=jax.ShapeDtypeStruct((n_seq * t, A_WIDTH), BF16),
        compiler_params=_params(("parallel",)),
        name="attn_context",
    )(sink, proj, proj)


def _rope(x, cos, sin_signed):
    w = x.shape[-1]
    lane = lax.broadcasted_iota(jnp.int32, x.shape, 1)
    swapped = jnp.where((lane % 32) < 16, pltpu.roll(x, w - 16, 1), pltpu.roll(x, 16, 1))
    return x * cos + swapped * sin_signed


def _attn_lat_kernel(sink_ref, q_ref, kv_ref, cq_ref, sq_ref, ck_ref, sk_ref, kc_ref, vc_ref, o_ref,
                     k_scr, v_scr, kc_scr, vc_scr, *, t):
    n = pl.program_id(1)
    span = ATTN_BLOCK + 2 * WINDOW

    @pl.when(n == 0)
    def _():
        k_scr[...] = _rope(kv_ref[:, :A_KV_WIDTH], ck_ref[...], sk_ref[...]).astype(BF16)
        v_scr[...] = kv_ref[:, A_KV_WIDTH:].astype(BF16)
        kc_scr[...] = kc_ref[0, 0].astype(BF16)
        vc_scr[...] = vc_ref[0, 0].astype(BF16)

    q = _rope(q_ref[...], cq_ref[...], sq_ref[...])
    start = pl.multiple_of(jnp.clip(n * ATTN_BLOCK - WINDOW, 0, t - span), ATTN_BLOCK)
    k_loc = k_scr[pl.ds(start, span), :]
    v_loc = v_scr[pl.ds(start, span), :]
    stacked = (A_GROUPS * ATTN_BLOCK, span)
    q_pos = n * ATTN_BLOCK + lax.broadcasted_iota(jnp.int32, stacked, 0) % ATTN_BLOCK
    k_pos = start + lax.broadcasted_iota(jnp.int32, stacked, 1)
    valid = jnp.logical_and(q_pos - k_pos <= WINDOW, k_pos - q_pos <= WINDOW)
    _gqa_attention(q, [(k_loc, v_loc, valid), (kc_scr[...], vc_scr[...], None)], sink_ref, o_ref)


def _attn_latent(proj, sink, cache_k, cache_v, layer, cos, sin, m_prompt, n_seq, t):
    kv_w = 2 * A_KV_WIDTH
    nb = t // ATTN_BLOCK
    past = cache_k.shape[2]
    row0 = m_prompt // ATTN_BLOCK
    seq0 = m_prompt // t
    return pl.pallas_call(
        functools.partial(_attn_lat_kernel, t=t),
        grid=(n_seq, nb),
        in_specs=[
            pl.BlockSpec(memory_space=pltpu.SMEM),
            pl.BlockSpec((ATTN_BLOCK, A_WIDTH), lambda b, n: (row0 + b * nb + n, CT_Q)),
            pl.BlockSpec((t, kv_w), lambda b, n: (seq0 + b, CT_MISC * COL_TILE // kv_w)),
            pl.BlockSpec((ATTN_BLOCK, A_WIDTH), lambda b, n: (n, 0)),
            pl.BlockSpec((ATTN_BLOCK, A_WIDTH), lambda b, n: (n, 0)),
            pl.BlockSpec((t, A_KV_WIDTH), lambda b, n: (0, 0)),
            pl.BlockSpec((t, A_KV_WIDTH), lambda b, n: (0, 0)),
            pl.BlockSpec((1, 1, past, A_KV_WIDTH), lambda b, n: (b, layer, 0, 0)),
            pl.BlockSpec((1, 1, past, A_KV_WIDTH), lambda b, n: (b, layer, 0, 0)),
        ],
        out_specs=pl.BlockSpec((ATTN_BLOCK, A_WIDTH), lambda b, n: (b * nb + n, 0)),
        out_shape=jax.ShapeDtypeStruct((n_seq * t, A_WIDTH), BF16),
        scratch_shapes=[
            pltpu.VMEM((t, A_KV_WIDTH), BF16),
            pltpu.VMEM((t, A_KV_WIDTH), BF16),
            pltpu.VMEM((past, A_KV_WIDTH), BF16),
            pltpu.VMEM((past, A_KV_WIDTH), BF16),
        ],
        compiler_params=_params(("parallel", "arbitrary")),
        name="attn_latent",
    )(sink, proj, proj, cos, sin, cos, sin, cache_k, cache_v)


def _rope_tables(t):
    pos = jnp.arange(t)
    half = HEAD_DIM // 2
    freqs = ROPE_BASE ** (-jnp.arange(0, half, 2, dtype=F32) / half)

    def tab(p):
        ang = p.astype(F32)[:, None] * freqs[None, :]
        c, s = jnp.cos(ang), jnp.sin(ang)
        return jnp.concatenate([c, c], -1), jnp.concatenate([-s, s], -1)

    c_row, s_row = tab(pos // GRID_W)
    c_col, s_col = tab(pos % GRID_W)
    cos = jnp.concatenate([c_row, c_col], -1)
    sin = jnp.concatenate([s_row, s_col], -1)
    return jnp.tile(cos, (1, A_HEADS)), jnp.tile(sin, (1, A_HEADS))


def _sgu_kernel(u_ref, v_ref, ng_ref, w_ref, b_ref, o_ref):
    for ch in range(SGU_TM // SGU_CHUNK):
        rs = slice(ch * SGU_CHUNK, (ch + 1) * SGU_CHUNK)
        for g in range(SGU_GROUPS):
            cs = slice(g * LANES, (g + 1) * LANES)
            vn = _layer_norm(_gelu(v_ref[rs, cs])) * ng_ref[:, cs]
            mixed = _dot(w_ref[g], vn.astype(BF16)) + b_ref[g]
            o_ref[rs, cs] = (_gelu(u_ref[rs, cs]) * mixed).astype(BF16)


def _sgu(proj, norm_g, w_s, bias):
    m = proj.shape[0]
    return pl.pallas_call(
        _sgu_kernel,
        grid=(m // SGU_TM,),
        in_specs=[
            pl.BlockSpec((SGU_TM, SGU_WIDTH), lambda i: (i, CT_SGU_U)),
            pl.BlockSpec((SGU_TM, SGU_WIDTH), lambda i: (i, CT_SGU_V)),
            pl.BlockSpec((1, SGU_WIDTH), lambda i: (0, 0)),
            pl.BlockSpec((SGU_GROUPS, SGU_CHUNK, SGU_CHUNK), lambda i: (0, 0, 0)),
            pl.BlockSpec((SGU_GROUPS, SGU_CHUNK, LANES), lambda i: (0, 0, 0)),
        ],
        out_specs=pl.BlockSpec((SGU_TM, SGU_WIDTH), lambda i: (i, 0)),
        out_shape=jax.ShapeDtypeStruct((m, SGU_WIDTH), BF16),
        compiler_params=_params(("parallel",)),
        name="sgu",
    )(proj, proj, norm_g, w_s, bias)


def _head_sums(x, ones_bd):
    return _dot(x, ones_bd, precision=HIGHEST)


def _rwkv_pre_kernel(x_ref, xp_ref, xn_ref, lo_ref, lop_ref, lon_ref, mu_ref, mul_ref, kk_ref, ka_ref,
                     rk_ref, w0_ref, a0_ref, w2_ref, a2_ref, g2_ref, ones_ref,
                     r_ref, v_ref, kkn_ref, gate_ref, bonus_ref,
                     lw0_ref, b0_ref, k0_ref, lw1_ref, b1_ref, k1_ref, *, n_prompt_tiles, tiles_per_seq):
    i = pl.program_id(0)
    is_prompt = i < n_prompt_tiles
    j = lax.rem(jnp.maximum(i - n_prompt_tiles, 0), tiles_per_seq)
    first = jnp.logical_or(is_prompt, j == 0)
    last = jnp.logical_or(is_prompt, j == tiles_per_seq - 1)

    def shift(x, xp, xn, mu):
        tm = x.shape[0]
        row = lax.broadcasted_iota(jnp.int32, x.shape, 0)
        prev_row = jnp.where(first, 0.0, xp[7:8, :])
        next_row = jnp.where(last, 0.0, xn[0:1, :])
        prev = jnp.where(row == 0, prev_row, pltpu.roll(x, 1, 0))
        nxt = jnp.where(row == tm - 1, next_row, pltpu.roll(x, tm - 1, 0))
        return x + mu[0:1, :] * (prev - x) + mu[1:2, :] * (nxt - x)

    xs = shift(x_ref[...], xp_ref[...], xn_ref[...], mu_ref[...])
    lo = shift(lo_ref[...], lop_ref[...], lon_ref[...], mul_ref[...])
    r = xs[:, :C_WIDTH]
    k = xs[:, C_WIDTH:2 * C_WIDTH]
    v = xs[:, 2 * C_WIDTH:]
    tw = jnp.tanh(lo[:, :W_LORA]).astype(BF16)
    al = lo[:, W_LORA:W_LORA + A_LORA].astype(BF16)
    gl = _sigmoid(lo[:, W_LORA + A_LORA:]).astype(BF16)
    ones_bd = ones_ref[...]

    r_ref[...] = r
    v_ref[...] = v
    gate_ref[...] = _dot(gl, g2_ref[...])
    kk0 = k * kk_ref[...]
    for p in range(HEAD_PAIRS):
        cs = slice(p * LANES, (p + 1) * LANES)
        ss = _head_sums(kk0[:, cs] * kk0[:, cs], ones_bd)
        kkn_ref[:, cs] = kk0[:, cs] * lax.rsqrt(jnp.maximum(ss, 1e-24))
    kkn = kkn_ref[...]

    bonus = None
    for d, (lw_ref, b_ref, kd_ref) in enumerate(((lw0_ref, b0_ref, k0_ref), (lw1_ref, b1_ref, k1_ref))):
        z = w0_ref[d:d + 1, :] + _dot(tw, w2_ref[d])
        softplus_neg = jnp.maximum(-z, 0.0) + jnp.log(1.0 + jnp.exp(-jnp.abs(z)))
        lw_ref[...] = -jnp.exp(-softplus_neg - 0.5)
        a = _sigmoid(a0_ref[d:d + 1, :] + _dot(al, a2_ref[d]))
        kd = k * (1.0 + (a - 1.0) * ka_ref[...])
        kd_ref[...] = kd
        b_ref[...] = kkn * a
        rkr = r * kd * rk_ref[...]
        parts = [_head_sums(rkr[:, p * LANES:(p + 1) * LANES], ones_bd) for p in range(HEAD_PAIRS)]
        term = jnp.concatenate(parts, axis=1) * v
        bonus = term if bonus is None else bonus + term
    bonus_ref[...] = bonus


def _rwkv_pre(proj, lp, ones_bd, m_prompt, t_sample):
    m = proj.shape[0]
    tm = PRE_TM
    rkv_w = 3 * C_WIDTH
    lo_blk = (CT_MISC * COL_TILE + 2 * A_KV_WIDTH) // LORA_COLS
    n8 = m // 8
    prev8 = lambda i: jnp.maximum(i * (tm // 8) - 1, 0)
    next8 = lambda i: jnp.minimum((i + 1) * (tm // 8), n8 - 1)
    full = lambda shape: pl.BlockSpec(shape, lambda i: (0,) * len(shape))
    out_spec = pl.BlockSpec((tm, C_WIDTH), lambda i: (i, 0))
    out_shape = jax.ShapeDtypeStruct((m, C_WIDTH), F32)
    kern = functools.partial(_rwkv_pre_kernel, n_prompt_tiles=m_prompt // tm, tiles_per_seq=t_sample // tm)
    return pl.pallas_call(
        kern,
        grid=(m // tm,),
        in_specs=[
            pl.BlockSpec((tm, rkv_w), lambda i: (i, CT_RKV * COL_TILE // rkv_w)),
            pl.BlockSpec((8, rkv_w), lambda i: (prev8(i), CT_RKV * COL_TILE // rkv_w)),
            pl.BlockSpec((8, rkv_w), lambda i: (next8(i), CT_RKV * COL_TILE // rkv_w)),
            pl.BlockSpec((tm, LORA_COLS), lambda i: (i, lo_blk)),
            pl.BlockSpec((8, LORA_COLS), lambda i: (prev8(i), lo_blk)),
            pl.BlockSpec((8, LORA_COLS), lambda i: (next8(i), lo_blk)),
            full((2, rkv_w)), full((2, LORA_COLS)), full((1, C_WIDTH)), full((1, C_WIDTH)),
            full((1, C_WIDTH)), full((2, C_WIDTH)), full((2, C_WIDTH)),
            full((2, W_LORA, C_WIDTH)), full((2, A_LORA, C_WIDTH)), full((G_LORA, C_WIDTH)),
            full((LANES, LANES)),
        ],
        out_specs=[out_spec] * 11,
        out_shape=[out_shape] * 11,
        compiler_params=_params(("parallel",)),
        name="rwkv_pre",
    )(proj, proj, proj, proj, proj, proj, lp["mu_rkv"], lp["mu_lora"], lp["k_k"], lp["k_a"], lp["r_k"],
      lp["w0"], lp["a0"], lp["w2"], lp["a2"], lp["g2"], ones_bd)


def _expand_heads(x, lane_lo):
    return jnp.concatenate([jnp.where(lane_lo, x, 0.0), jnp.where(lane_lo, 0.0, x)], axis=0).astype(BF16)


def _scan_masks(reverse):
    c = SCAN_CHUNK
    ri = lax.broadcasted_iota(jnp.int32, (2 * c, 2 * c), 0)
    ci = lax.broadcasted_iota(jnp.int32, (2 * c, 2 * c), 1)
    same = (ri // c) == (ci // c)
    tr, tc = ri % c, ci % c
    strict = jnp.logical_and(same, (tc > tr) if reverse else (tc < tr))
    incl = jnp.logical_and(same, (tc >= tr) if reverse else (tc <= tr))
    levels = []
    s = 1
    while s < c:
        late, early = (ci, ri) if reverse else (ri, ci)
        blk = jnp.logical_and((ri // (2 * s)) == (ci // (2 * s)),
                              jnp.logical_and(late % (2 * s) >= s, early % (2 * s) < s))
        levels.append(blk)
        s *= 2
    eye = ri == ci
    return strict, incl, levels, eye


def _scan_chunks(chains):
    c = SCAN_CHUNK
    n = 2 * c
    lane_lo = lax.broadcasted_iota(jnp.int32, (c, LANES), 1) < HEAD_DIM
    n_levels = len(chains[0][8][2])

    prep = []
    for r, v, kk, lw, b, k, cum, state, masks, reverse in chains:
        cum_end = cum[0:1, :] if reverse else cum[c - 1:c, :]
        inv = jnp.exp(-cum)
        tail = jnp.exp(cum_end - cum)
        ar = jnp.concatenate([_expand_heads(kk * jnp.exp(cum - lw), lane_lo),
                              _expand_heads(r * jnp.exp(cum), lane_lo)], axis=0)
        bk = jnp.concatenate([_expand_heads(b * inv, lane_lo), _expand_heads(k * inv, lane_lo)], axis=0)
        bk_tail = jnp.concatenate([_expand_heads(b * tail, lane_lo), _expand_heads(k * tail, lane_lo)], axis=0)
        prep.append((ar, bk, bk_tail, _expand_heads(v, lane_lo), jnp.exp(cum_end)))

    gs = [_dot_nt(ar, bk) for ar, bk, _, _, _ in prep]
    xs = [_dot_nt(p[0], ch[7].astype(BF16)) for p, ch in zip(prep, chains)]
    rhs = [x[:n] + _dot(jnp.where(ch[8][0], g[:n, n:], 0.0).astype(BF16), p[3])
           for x, g, p, ch in zip(xs, gs, prep, chains)]
    t_inv = [jnp.where(ch[8][2][0], -g[:n, :n], jnp.where(ch[8][3], 1.0, 0.0)) for g, ch in zip(gs, chains)]
    for lvl in range(1, n_levels):
        t_b = [t.astype(BF16) for t in t_inv]
        half = [_dot(tb, jnp.where(ch[8][2][lvl], g[:n, :n], 0.0).astype(BF16)).astype(BF16)
                for tb, g, ch in zip(t_b, gs, chains)]
        t_inv = [t - _dot(h, tb) for t, h, tb in zip(t_inv, half, t_b)]
    us = [-_dot(t.astype(BF16), z.astype(BF16)) for t, z in zip(t_inv, rhs)]
    uvs = [jnp.concatenate([u.astype(BF16), p[3]], axis=0) for u, p in zip(us, prep)]
    out = []
    for x, g, uv, p, ch in zip(xs, gs, uvs, prep, chains):
        incl = ch[8][1]
        d_r = jnp.concatenate([jnp.where(incl, g[n:, :n], 0.0), jnp.where(incl, g[n:, n:], 0.0)], axis=1)
        ye = x[n:] + _dot(d_r.astype(BF16), uv)
        out.append(ye[:c] + ye[c:])
    new_states = [ch[7] * p[4] + _dot_tn(uv, p[2]) for uv, p, ch in zip(uvs, prep, chains)]
    return list(zip(out, new_states))


def _scan_kernel(*refs, zero_init):
    fwd, bwd, (tri_f, tri_b) = refs[0:6], refs[6:12], refs[12:14]
    if zero_init:
        yf_ref, yb_ref, sfin_ref, s_ref = refs[14:18]
    else:
        s0_ref, yf_ref, yb_ref, s_ref = refs[14:18]
    head_block = lambda h: (slice(h * HEAD_DIM, (h + 1) * HEAD_DIM),) * 2

    @pl.when(pl.program_id(1) == 0)
    def _():
        s_ref[...] = jnp.zeros(s_ref.shape, F32)
        if not zero_init:
            for d in range(2):
                for p in range(HEAD_PAIRS):
                    for h in range(2):
                        s_ref[(d, p) + head_block(h)] = s0_ref[0, d, 2 * p + h]

    chains, dests = [], []
    for d, (ins, tri, y_ref) in enumerate(((fwd, tri_f, yf_ref), (bwd, tri_b, yb_ref))):
        r_ref, v_ref, kk_ref, lw_ref, b_ref, k_ref = ins
        cum_all = _dot(tri[...], lw_ref[...], precision=HIGHEST)
        masks = _scan_masks(reverse=(d == 1))
        for p in range(HEAD_PAIRS):
            cs = slice(p * LANES, (p + 1) * LANES)
            chains.append((r_ref[:, cs], v_ref[:, cs], kk_ref[:, cs], lw_ref[:, cs], b_ref[:, cs], k_ref[:, cs],
                           cum_all[:, cs], s_ref[d, p], masks, d == 1))
            dests.append((y_ref, d, p, cs))
    for (y_ref, d, p, cs), (y, new_state) in zip(dests, _scan_chunks(chains)):
        y_ref[:, cs] = y
        s_ref[d, p] = new_state

    if zero_init:
        @pl.when(pl.program_id(1) == pl.num_programs(1) - 1)
        def _():
            for d in range(2):
                for p in range(HEAD_PAIRS):
                    for h in range(2):
                        sfin_ref[0, d, 2 * p + h] = s_ref[(d, p) + head_block(h)]


def _rwkv_scan(pre, states, layer, row0, n_seq, t):
    r, v, kk, _, _, lw0, b0, k0, lw1, b1, k1 = pre
    c = SCAN_CHUNK
    nc = t // c
    blk0 = row0 // c
    fspec = pl.BlockSpec((c, C_WIDTH), lambda s, j: (blk0 + s * nc + j, 0))
    bspec = pl.BlockSpec((c, C_WIDTH), lambda s, j: (blk0 + s * nc + nc - 1 - j, 0))
    sblock = (1, 2, C_HEADS, HEAD_DIM, HEAD_DIM)
    tspec = pl.BlockSpec((c, c), lambda s, j: (0, 0))
    tri_f = jnp.tril(jnp.ones((c, c), F32))
    args = [r, v, kk, lw0, b0, k0, r, v, kk, lw1, b1, k1, tri_f, tri_f.T]
    in_specs = [fspec] * 6 + [bspec] * 6 + [tspec, tspec]
    out_specs = [pl.BlockSpec((c, C_WIDTH), lambda s, j: (s * nc + j, 0)),
                 pl.BlockSpec((c, C_WIDTH), lambda s, j: (s * nc + nc - 1 - j, 0))]
    out_shape = [jax.ShapeDtypeStruct((n_seq * t, C_WIDTH), F32)] * 2
    if states is None:
        out_specs.append(pl.BlockSpec(sblock, lambda s, j: (s, 0, 0, 0, 0)))
        out_shape.append(jax.ShapeDtypeStruct((n_seq,) + sblock[1:], F32))
    else:
        args.append(states)
        in_specs.append(pl.BlockSpec(sblock, lambda s, j: (s, layer, 0, 0, 0)))
    return pl.pallas_call(
        functools.partial(_scan_kernel, zero_init=states is None),
        grid=(n_seq, nc),
        in_specs=in_specs,
        out_specs=out_specs,
        out_shape=out_shape,
        scratch_shapes=[pltpu.VMEM((2, HEAD_PAIRS, LANES, LANES), F32)],
        compiler_params=_params(("parallel", "arbitrary")),
        name="rwkv_scan",
    )(*args)


def _rwkv_post_kernel(yfp_ref, ybp_ref, yfs_ref, ybs_ref, bonus_ref, gate_ref, g_ref, b_ref, ones_ref, o_ref,
                      y_scr, *, n_prompt_tiles):
    i = pl.program_id(0)

    @pl.when(i < n_prompt_tiles)
    def _():
        y_scr[...] = yfp_ref[...] + ybp_ref[...]

    @pl.when(i >= n_prompt_tiles)
    def _():
        y_scr[...] = yfs_ref[...] + ybs_ref[...]

    ones_bd = ones_ref[...]
    for p in range(HEAD_PAIRS):
        cs = slice(p * LANES, (p + 1) * LANES)
        y = y_scr[:, cs]
        mu = _head_sums(y, ones_bd) * (1.0 / HEAD_DIM)
        yc = y - mu
        var = _head_sums(yc * yc, ones_bd) * (1.0 / HEAD_DIM)
        yn = yc * lax.rsqrt(var + GN_EPS) * g_ref[:, cs] + b_ref[:, cs]
        o_ref[:, cs] = ((yn + bonus_ref[:, cs]) * gate_ref[:, cs]).astype(BF16)


def _rwkv_post(y_prompt, y_sample, bonus, gate, ln_g, ln_b, ones_bd):
    m = bonus.shape[0]
    n_p = y_prompt[0].shape[0] // POST_TM
    spec = pl.BlockSpec((POST_TM, C_WIDTH), lambda i: (i, 0))
    pspec = pl.BlockSpec((POST_TM, C_WIDTH), lambda i: (jnp.minimum(i, n_p - 1), 0))
    sspec = pl.BlockSpec((POST_TM, C_WIDTH), lambda i: (jnp.maximum(i - n_p, 0), 0))
    vec = pl.BlockSpec((1, C_WIDTH), lambda i: (0, 0))
    return pl.pallas_call(
        functools.partial(_rwkv_post_kernel, n_prompt_tiles=n_p),
        grid=(m // POST_TM,),
        in_specs=[pspec, pspec, sspec, sspec, spec, spec, vec, vec, pl.BlockSpec((LANES, LANES), lambda i: (0, 0))],
        out_specs=spec,
        out_shape=jax.ShapeDtypeStruct((m, C_WIDTH), BF16),
        scratch_shapes=[pltpu.VMEM((POST_TM, C_WIDTH), F32)],
        compiler_params=_params(("parallel",)),
        name="rwkv_post",
    )(*y_prompt, *y_sample, bonus, gate, ln_g, ln_b, ones_bd)


def _merge_kernel(x_ref, attn_p_ref, attn_s_ref, sgu_ref, rw_ref, ga_ref, gb_ref, gc_ref, bg_ref, g1_ref,
                  wa_ref, wb_ref, wc_ref, wo_ref, lg_ref, lb_ref, o_ref, attn_scr, *, n_prompt_tiles):
    i = pl.program_id(0)

    @pl.when(i < n_prompt_tiles)
    def _():
        attn_scr[...] = attn_p_ref[...]

    @pl.when(i >= n_prompt_tiles)
    def _():
        attn_scr[...] = attn_s_ref[...]

    merged = None
    for br, (a_ref, w_ref, g_ref) in enumerate(((attn_scr, wa_ref, ga_ref), (sgu_ref, wb_ref, gb_ref),
                                                 (rw_ref, wc_ref, gc_ref))):
        gate = _sigmoid(g_ref[...] + bg_ref[:, br * D_MODEL:(br + 1) * D_MODEL])
        term = gate * _dot(a_ref[...], w_ref[...])
        merged = term if merged is None else merged + term
    out = _dot(merged.astype(BF16), wo_ref[...])
    y = _layer_norm(ALPHA * x_ref[...] + g1_ref[0] * out)
    o_ref[...] = y * lg_ref[...] + lb_ref[...]


def _merge(x, attn_p, attn_s, sgu, rw, proj, mod3, lp, m_prompt, t_sample):
    m = x.shape[0]
    tm = MERGE_TM
    n_p = m_prompt // tm
    req = functools.partial(_request_of_tile, tm=tm, m_prompt=m_prompt, t_sample=t_sample)
    row = lambda w: pl.BlockSpec((tm, w), lambda i: (i, 0))
    const = lambda shape: pl.BlockSpec(shape, lambda i: (0,) * len(shape), pipeline_mode=pl.Buffered(1))
    gate = lambda br: pl.BlockSpec((tm, D_MODEL), lambda i: (i, CT_GATE * COL_TILE // D_MODEL + br))
    return pl.pallas_call(
        functools.partial(_merge_kernel, n_prompt_tiles=n_p),
        grid=(m // tm,),
        in_specs=[
            row(D_MODEL),
            pl.BlockSpec((tm, A_WIDTH), lambda i: (jnp.minimum(i, n_p - 1), 0)),
            pl.BlockSpec((tm, A_WIDTH), lambda i: (jnp.maximum(i - n_p, 0), 0)),
            row(SGU_WIDTH), row(C_WIDTH), gate(0), gate(1), gate(2),
            const((1, N_BRANCH * D_MODEL)),
            pl.BlockSpec((1, 1, D_MODEL), lambda i: (req(i) * 6 + 2, 0, 0)),
            const((A_WIDTH, D_MODEL)), const((SGU_WIDTH, D_MODEL)), const((C_WIDTH, D_MODEL)),
            const((D_MODEL, D_MODEL)), const((1, D_MODEL)), const((1, D_MODEL)),
        ],
        out_specs=row(D_MODEL),
        out_shape=jax.ShapeDtypeStruct((m, D_MODEL), F32),
        scratch_shapes=[pltpu.VMEM((tm, A_WIDTH), BF16)],
        compiler_params=_params(("parallel",)),
        name="merge",
    )(x, attn_p, attn_s, sgu, rw, proj, proj, proj, lp["b_gate"], mod3, lp["w_branch_a"], lp["w_branch_b"],
      lp["w_branch_c"], lp["w_out"], lp["ln1_g"], lp["ln1_b"])


def _ffn_kernel(x_ref, sh_ref, sc_ref, g2_ref, wg_ref, wu_ref, wo_ref, lg_ref, lb_ref, o_ref, h_scr):
    f = pl.program_id(1)

    @pl.when(f == 0)
    def _():
        h_scr[...] = (_layer_norm(x_ref[...]) * (1.0 + sc_ref[0]) + sh_ref[0]).astype(BF16)
        o_ref[...] = jnp.zeros(o_ref.shape, F32)

    h = h_scr[...]
    g = _dot(h, wg_ref[...])
    u = _dot(h, wu_ref[...])
    o_ref[...] += _dot((g * _sigmoid(g) * u).astype(BF16), wo_ref[...])

    @pl.when(f == pl.num_programs(1) - 1)
    def _():
        y = _layer_norm(ALPHA * x_ref[...] + g2_ref[0] * o_ref[...])
        o_ref[...] = y * lg_ref[...] + lb_ref[...]


def _ffn_dense(x, mod3, w_in, w_out, ln_g, ln_b, m_prompt, t_sample):
    m = x.shape[0]
    tm, tf = FFN_TM, FFN_TF
    nf = D_FF // tf
    req = functools.partial(_request_of_tile, tm=tm, m_prompt=m_prompt, t_sample=t_sample)
    modspec = lambda k: pl.BlockSpec((1, 1, D_MODEL), lambda i, f: (req(i) * 6 + k, 0, 0))
    vec = pl.BlockSpec((1, D_MODEL), lambda i, f: (0, 0))
    return pl.pallas_call(
        _ffn_kernel,
        grid=(m // tm, nf),
        in_specs=[
            pl.BlockSpec((tm, D_MODEL), lambda i, f: (i, 0)),
            modspec(3), modspec(4), modspec(5),
            pl.BlockSpec((D_MODEL, tf), lambda i, f: (0, f)),
            pl.BlockSpec((D_MODEL, tf), lambda i, f: (0, nf + f)),
            pl.BlockSpec((tf, D_MODEL), lambda i, f: (f, 0)),
            vec, vec,
        ],
        out_specs=pl.BlockSpec((tm, D_MODEL), lambda i, f: (i, 0), pipeline_mode=pl.Buffered(1)),
        out_shape=jax.ShapeDtypeStruct((m, D_MODEL), F32),
        scratch_shapes=[pltpu.VMEM((tm, D_MODEL), BF16)],
        compiler_params=_params(("parallel", "arbitrary")),
        name="ffn_dense",
    )(x, mod3, mod3, mod3, w_in, w_in, w_out, ln_g, ln_b)


def _route_kernel(x_ref, sh_ref, sc_ref, wr_ref, h_ref, idx_ref, gate_ref):
    h = _layer_norm(x_ref[...]) * (1.0 + sc_ref[0]) + sh_ref[0]
    h_ref[...] = h
    logits = _dot(h, wr_ref[...], precision=HIGHEST)
    lane = lax.broadcasted_iota(jnp.int32, logits.shape, 1)
    logits = jnp.where(lane < N_EXPERTS, logits, -jnp.inf)
    lane_f = lane.astype(F32)
    m1 = jnp.max(logits, axis=-1, keepdims=True)
    i1 = jnp.min(jnp.where(logits == m1, lane_f, float(LANES)), axis=-1, keepdims=True)
    rest = jnp.where(lane_f == i1, -jnp.inf, logits)
    m2 = jnp.max(rest, axis=-1, keepdims=True)
    i2 = jnp.min(jnp.where(rest == m2, lane_f, float(LANES)), axis=-1, keepdims=True)
    e = jnp.exp(m2 - m1)
    g1 = 1.0 / (1.0 + e)
    idx_ref[...] = jnp.where(lane == 0, i1, jnp.where(lane == 1, i2, 0.0)).astype(jnp.int32)
    gate_ref[...] = jnp.where(lane == 0, g1, jnp.where(lane == 1, e * g1, 0.0))


def _route(x, mod3, router_p, m_prompt, t_sample):
    m = x.shape[0]
    tm = ROUTE_TM
    req = functools.partial(_request_of_tile, tm=tm, m_prompt=m_prompt, t_sample=t_sample)
    modspec = lambda k: pl.BlockSpec((1, 1, D_MODEL), lambda i: (req(i) * 6 + k, 0, 0))
    return pl.pallas_call(
        _route_kernel,
        grid=(m // tm,),
        in_specs=[pl.BlockSpec((tm, D_MODEL), lambda i: (i, 0)), modspec(3), modspec(4),
                  pl.BlockSpec((D_MODEL, LANES), lambda i: (0, 0))],
        out_specs=[pl.BlockSpec((tm, D_MODEL), lambda i: (i, 0)), pl.BlockSpec((tm, LANES), lambda i: (i, 0)),
                   pl.BlockSpec((tm, LANES), lambda i: (i, 0))],
        out_shape=[jax.ShapeDtypeStruct((m, D_MODEL), F32), jax.ShapeDtypeStruct((m, LANES), jnp.int32),
                   jax.ShapeDtypeStruct((m, LANES), F32)],
        compiler_params=_params(("parallel",)),
        name="route",
    )(x, mod3, mod3, router_p)


def _row_copy(src_hbm, row, dst, dst_row, sem):
    return pltpu.make_async_copy(src_hbm.at[pl.ds(row, 1)], dst.at[pl.ds(dst_row, 1)], sem)


def _moe_kernel(tile_e_ref, tile_valid_ref, row_tok_ref, row_tok_next_ref, h_hbm, wg_ref, wu_ref, wo_ref, o_ref,
                land_scr, x_scr, sem):
    i = pl.program_id(0)
    f = pl.program_id(1)
    valid = tile_valid_ref[i]

    def gather(tok_ref):
        def start(r2, carry):
            for prio in range(2):
                r = 2 * r2 + prio
                _row_copy(h_hbm, tok_ref[0, 0, r], land_scr, r, sem).start(priority=prio)
            return carry
        lax.fori_loop(0, MOE_TM // 2, start, 0)

    @pl.when(f == 0)
    def _():
        o_ref[...] = jnp.zeros(o_ref.shape, F32)

        @pl.when(valid > 0)
        def _():
            @pl.when(i == 0)
            def _():
                gather(row_tok_ref)

            def wait(r, carry):
                _row_copy(h_hbm, 0, land_scr, r, sem).wait()
                return carry

            lax.fori_loop(0, MOE_TM, wait, 0)
            x_scr[...] = land_scr[...].astype(BF16)

    next_i = jnp.minimum(i + 1, pl.num_programs(0) - 1)

    @pl.when(jnp.logical_and(f == 1, jnp.logical_and(i + 1 < pl.num_programs(0), tile_valid_ref[next_i] > 0)))
    def _():
        gather(row_tok_next_ref)

    @pl.when(valid > 0)
    def _():
        n_sub = lax.shift_right_logical(valid + (MOE_SUB - 1), MOE_SUB.bit_length() - 1)
        for nb in range(1, MOE_TM // MOE_SUB + 1):
            @pl.when(n_sub == nb)
            def _():
                rows = nb * MOE_SUB
                x = x_scr[:rows, :]
                g = _dot(x, wg_ref[0].astype(BF16))
                u = _dot(x, wu_ref[0].astype(BF16))
                o_ref[:rows, :] += _dot((g * _sigmoid(g) * u).astype(BF16), wo_ref[0].astype(BF16))


def _moe_experts(h, row_tok, tile_e, tile_valid, w_in, w_out):
    n_tiles = tile_e.shape[0]
    nf = D_FF_EXPERT // MOE_TF
    fidx = lambda i, f, tv: jnp.where(tv[i] > 0, f, nf - 1)
    grid_spec = pltpu.PrefetchScalarGridSpec(
        num_scalar_prefetch=2,
        grid=(n_tiles, nf),
        in_specs=[
            pl.BlockSpec((1, 1, MOE_TM), lambda i, f, te, tv: (i, 0, 0), memory_space=pltpu.SMEM),
            pl.BlockSpec((1, 1, MOE_TM), lambda i, f, te, tv: (jnp.minimum(i + 1, n_tiles - 1), 0, 0),
                         memory_space=pltpu.SMEM),
            pl.BlockSpec(memory_space=pl.ANY),
            pl.BlockSpec((1, D_MODEL, MOE_TF), lambda i, f, te, tv: (te[i], 0, fidx(i, f, tv))),
            pl.BlockSpec((1, D_MODEL, MOE_TF), lambda i, f, te, tv: (te[i], 0, nf + fidx(i, f, tv))),
            pl.BlockSpec((1, MOE_TF, D_MODEL), lambda i, f, te, tv: (te[i], fidx(i, f, tv), 0)),
        ],
        out_specs=pl.BlockSpec((MOE_TM, D_MODEL), lambda i, f, te, tv: (i, 0), pipeline_mode=pl.Buffered(1)),
        scratch_shapes=[pltpu.VMEM((MOE_TM, D_MODEL), F32), pltpu.VMEM((MOE_TM, D_MODEL), BF16),
                        pltpu.SemaphoreType.DMA(())],
    )
    row_tok3 = row_tok.reshape(n_tiles, 1, MOE_TM)
    return pl.pallas_call(
        _moe_kernel,
        grid_spec=grid_spec,
        out_shape=jax.ShapeDtypeStruct((n_tiles * MOE_TM, D_MODEL), F32),
        compiler_params=_params(("arbitrary", "arbitrary")),
        name="moe_experts",
    )(tile_e, tile_valid, row_tok3, row_tok3, h, w_in, w_in, w_out)


def _combine_kernel(pos_ref, pos_next_ref, ys_hbm, x_ref, gate_ref, g2_ref, lg_ref, lb_ref, o_ref, buf, sems):
    tm = COMB_TM
    i = pl.program_id(0)
    slot = lax.rem(i, 2)

    def gather(p_ref, s):
        def start(t, carry):
            for k in range(TOP_K):
                _row_copy(ys_hbm, p_ref[0, 0, t * TOP_K + k], buf.at[s, k], t, sems.at[s]).start(priority=k)
            return carry
        lax.fori_loop(0, tm, start, 0)

    @pl.when(i == 0)
    def _():
        gather(pos_ref, 0)

    @pl.when(i + 1 < pl.num_programs(0))
    def _():
        gather(pos_next_ref, 1 - slot)

    def wait(t, carry):
        for k in range(TOP_K):
            _row_copy(ys_hbm, 0, buf.at[slot, k], t, sems.at[slot]).wait()
        return carry

    lax.fori_loop(0, tm, wait, 0)
    y = gate_ref[:, 0:1] * buf[slot, 0] + gate_ref[:, 1:2] * buf[slot, 1]
    z = _layer_norm(ALPHA * x_ref[...] + g2_ref[0] * y)
    o_ref[...] = z * lg_ref[...] + lb_ref[...]


def _moe_combine(x, ys, pos, gates, mod3, ln_g, ln_b, m_prompt, t_sample):
    m = x.shape[0]
    tm = COMB_TM
    n = m // tm
    req = functools.partial(_request_of_tile, tm=tm, m_prompt=m_prompt, t_sample=t_sample)
    pos3 = pos.reshape(n, 1, tm * TOP_K)
    return pl.pallas_call(
        _combine_kernel,
        grid=(n,),
        in_specs=[
            pl.BlockSpec((1, 1, tm * TOP_K), lambda i: (i, 0, 0), memory_space=pltpu.SMEM),
            pl.BlockSpec((1, 1, tm * TOP_K), lambda i: (jnp.minimum(i + 1, n - 1), 0, 0), memory_space=pltpu.SMEM),
            pl.BlockSpec(memory_space=pl.ANY),
            pl.BlockSpec((tm, D_MODEL), lambda i: (i, 0)),
            pl.BlockSpec((tm, LANES), lambda i: (i, 0)),
            pl.BlockSpec((1, 1, D_MODEL), lambda i: (req(i) * 6 + 5, 0, 0)),
            pl.BlockSpec((1, D_MODEL), lambda i: (0, 0)),
            pl.BlockSpec((1, D_MODEL), lambda i: (0, 0)),
        ],
        out_specs=pl.BlockSpec((tm, D_MODEL), lambda i: (i, 0)),
        out_shape=jax.ShapeDtypeStruct((m, D_MODEL), F32),
        scratch_shapes=[pltpu.VMEM((2, TOP_K, tm, D_MODEL), F32), pltpu.SemaphoreType.DMA((2,))],
        compiler_params=_params(("arbitrary",)),
        name="moe_combine",
    )(pos3, pos3, ys, x, gates, mod3, ln_g, ln_b)


def _moe_plan(idx):
    m = idx.shape[0]
    flat_e = idx.reshape(-1)
    n_assign = m * TOP_K
    onehot = (flat_e[:, None] == jnp.arange(N_EXPERTS, dtype=jnp.int32)[None, :]).astype(jnp.int32)
    csum = jnp.cumsum(onehot, axis=0)
    counts = csum[-1]
    rank = jnp.take_along_axis(csum, flat_e[:, None], axis=1)[:, 0] - 1
    padded = (counts + MOE_TM - 1) // MOE_TM * MOE_TM
    pad_end = jnp.cumsum(padded)
    pad_start = pad_end - padded
    pos = (pad_start[flat_e] + rank).astype(jnp.int32)
    n_tiles = n_assign // MOE_TM + N_EXPERTS
    row_tok = jnp.zeros((n_tiles * MOE_TM,), jnp.int32).at[pos].set(jnp.arange(n_assign, dtype=jnp.int32) // TOP_K)
    tile_start = jnp.arange(n_tiles, dtype=jnp.int32) * MOE_TM
    tile_e = jnp.sum((pad_end[None, :] <= tile_start[:, None]).astype(jnp.int32), axis=1)
    used = tile_e < N_EXPERTS
    last_e = jnp.max(jnp.where(counts > 0, jnp.arange(N_EXPERTS, dtype=jnp.int32), 0))
    tile_e = jnp.where(used, tile_e, last_e)
    seg_end = pad_start[tile_e] + counts[tile_e]
    tile_valid = jnp.where(used, jnp.clip(seg_end - tile_start, 0, MOE_TM), 0).astype(jnp.int32)
    return row_tok, tile_e, tile_valid, pos


def _reorder_w_in(w_in):
    o_q, o_k, o_v = 0, A_WIDTH, A_WIDTH + A_KV_WIDTH
    o_xb = o_v + A_KV_WIDTH
    o_xc = o_xb + 2 * SGU_WIDTH
    o_lora = o_xc + 3 * C_WIDTH
    o_gl = o_xc + C_COLS
    cols = lambda a, n: w_in[:, :, a:a + n]
    pad = jnp.zeros(w_in.shape[:2] + (COL_TILE - 2 * A_KV_WIDTH - LORA_COLS,), w_in.dtype)
    parts = [cols(o_gl, N_BRANCH * D_MODEL), cols(o_xb, 2 * SGU_WIDTH), cols(o_q, A_WIDTH),
             cols(o_xc, 3 * C_WIDTH), cols(o_k, 2 * A_KV_WIDTH), cols(o_lora, LORA_COLS), pad]
    return jnp.concatenate(parts, axis=-1).astype(BF16)


def kernel(x_prompt, x_sample, cache_k, cache_v, state_rwkv, c, c_ctx, w_mod, b_mod, w_in, b_gate, attn_sink, sgu_norm_g, sgu_w, sgu_b, rwkv_mu, rwkv_w0, rwkv_w2, rwkv_a0, rwkv_a2, rwkv_g2, rwkv_k_k, rwkv_k_a, rwkv_r_k, rwkv_ln_g, rwkv_ln_b, w_branch_a, w_branch_b, w_branch_c, w_out, ln1_g, ln1_b, ln2_g, ln2_b, ffn_w_in, ffn_w_out, moe_router, moe_w_in, moe_w_out):
    bp, tp, _ = x_prompt.shape
    bs, ts, _ = x_sample.shape
    past = cache_k.shape[2]
    mp, ms = bp * tp, bs * ts
    assert bs + 1 <= 16 and mp % IN_TM == 0 and ts % IN_TM == 0 and tp % PRE_TM == 0 and IN_TM % tp == 0

    x = jnp.concatenate([x_prompt.reshape(mp, D_MODEL), x_sample.reshape(ms, D_MODEL)], axis=0)
    cond = jnp.zeros((16, D_MODEL), F32).at[0].set(c_ctx).at[1:1 + bs].set(c)
    mod = _modulation(cond, w_mod, b_mod)

    w_in_p = _reorder_w_in(w_in)
    cos, sin = _rope_tables(ts)
    ones_bd = jnp.kron(jnp.eye(LANES // HEAD_DIM, dtype=F32), jnp.ones((HEAD_DIM, HEAD_DIM), F32))
    cache_k4 = cache_k.reshape(bs, DEPTH, past, A_KV_WIDTH)
    cache_v4 = cache_v.reshape(bs, DEPTH, past, A_KV_WIDTH)
    states4 = state_rwkv.astype(F32).reshape(bs, DEPTH * 2, C_HEADS, HEAD_DIM, HEAD_DIM)

    new_k, new_v, new_s = [], [], []
    for l in range(DEPTH):
        mod3 = mod[l].reshape(16 * 6, 1, D_MODEL)
        lp = {
            "mu_rkv": rwkv_mu[l][:, :3 * C_WIDTH], "mu_lora": rwkv_mu[l][:, 3 * C_WIDTH:],
            "k_k": rwkv_k_k[l].reshape(1, C_WIDTH), "k_a": rwkv_k_a[l].reshape(1, C_WIDTH),
            "r_k": rwkv_r_k[l].reshape(1, C_WIDTH), "w0": rwkv_w0[l], "a0": rwkv_a0[l],
            "w2": rwkv_w2[l].astype(BF16), "a2": rwkv_a2[l].astype(BF16), "g2": rwkv_g2[l].astype(BF16),
            "b_gate": b_gate[l].reshape(1, N_BRANCH * D_MODEL),
            "w_branch_a": w_branch_a[l].astype(BF16), "w_branch_b": w_branch_b[l].astype(BF16),
            "w_branch_c": w_branch_c[l].astype(BF16), "w_out": w_out[l].astype(BF16),
            "ln1_g": ln1_g[l].reshape(1, D_MODEL), "ln1_b": ln1_b[l].reshape(1, D_MODEL),
        }
        proj = _in_proj(x, mod3, w_in_p[l], mp, ts)

        attn_p = _attn_context(proj, attn_sink[l], bp, tp)
        attn_s = _attn_latent(proj, attn_sink[l], cache_k4, cache_v4, l, cos, sin, mp, bs, ts)
        kv_cols = proj[:mp, CT_MISC * COL_TILE:CT_MISC * COL_TILE + 2 * A_KV_WIDTH]
        new_k.append(kv_cols[:, :A_KV_WIDTH].reshape(bp, tp, A_KV_HEADS, HEAD_DIM))
        new_v.append(kv_cols[:, A_KV_WIDTH:].reshape(bp, tp, A_KV_HEADS, HEAD_DIM))

        sgu_bias = jnp.broadcast_to(sgu_b[l][:, :, None], (SGU_GROUPS, SGU_CHUNK, LANES))
        sgu = _sgu(proj, sgu_norm_g[l].reshape(1, SGU_WIDTH), sgu_w[l].astype(BF16), sgu_bias)

        pre = _rwkv_pre(proj, lp, ones_bd, mp, ts)
        yf_p, yb_p, s_fin = _rwkv_scan(pre, None, l, 0, bp, tp)
        yf_s, yb_s = _rwkv_scan(pre, states4, l, mp, bs, ts)
        rw = _rwkv_post((yf_p, yb_p), (yf_s, yb_s), pre[4], pre[3],
                        rwkv_ln_g[l].reshape(1, C_WIDTH), rwkv_ln_b[l].reshape(1, C_WIDTH), ones_bd)
        new_s.append(s_fin)

        x1 = _merge(x, attn_p, attn_s, sgu, rw, proj, mod3, lp, mp, ts)

        ln2g, ln2b = ln2_g[l].reshape(1, D_MODEL), ln2_b[l].reshape(1, D_MODEL)
        if l % 2 == 0:
            x = _ffn_dense(x1, mod3, ffn_w_in[l // 2].astype(BF16), ffn_w_out[l // 2].astype(BF16), ln2g, ln2b, mp, ts)
        else:
            router_p = jnp.zeros((D_MODEL, LANES), F32).at[:, :N_EXPERTS].set(moe_router[l // 2])
            h, idx, gates = _route(x1, mod3, router_p, mp, ts)
            row_tok, tile_e, tile_valid, pos = _moe_plan(idx[:, :TOP_K])
            ys = _moe_experts(h, row_tok, tile_e, tile_valid, moe_w_in[l // 2], moe_w_out[l // 2])
            x = _moe_combine(x1, ys, pos, gates, mod3, ln2g, ln2b, mp, ts)

    y_prompt = x[:mp].reshape(bp, tp, D_MODEL)
    y_sample = x[mp:].reshape(bs, ts, D_MODEL)
    return (y_prompt, y_sample, jnp.stack(new_k, axis=1), jnp.stack(new_v, axis=1), jnp.stack(new_s, axis=1))
```

```python
import functools

import jax
import jax.numpy as jnp
from jax import lax
from jax.experimental import pallas as pl
from jax.experimental.pallas import tpu as pltpu

F32 = jnp.float32
BF16 = jnp.bfloat16
HIGHEST = lax.Precision.HIGHEST

D_MODEL = 2048
DEPTH = 2
GRID_W = 64
A_HEADS = 16
A_KV_HEADS = 4
A_GROUPS = A_HEADS // A_KV_HEADS
HEAD_DIM = 64
A_WIDTH = A_HEADS * HEAD_DIM
A_KV_WIDTH = A_KV_HEADS * HEAD_DIM
ATTN_SCALE = HEAD_DIM ** -0.5
WINDOW = 128
ATTN_BLOCK = 128
ROPE_BASE = 10000.0
SGU_CHUNK = 128
SGU_GROUPS = 8
SGU_WIDTH = 1024
C_HEADS = 16
C_WIDTH = C_HEADS * HEAD_DIM
W_LORA = 64
A_LORA = 64
G_LORA = 128
LORA_COLS = W_LORA + A_LORA + G_LORA
C_COLS = 3 * C_WIDTH + LORA_COLS
N_BRANCH = 3
D_FF = 5632
N_EXPERTS = 8
TOP_K = 2
D_FF_EXPERT = 7168
ALPHA = (2 * DEPTH) ** 0.25
LN_EPS = 1e-6
GN_EPS = HEAD_DIM * 1e-5
NEG_INF = -1e30

LANES = 128
HEAD_PAIRS = C_HEADS * HEAD_DIM // LANES
SCAN_CHUNK = 64
VMEM_LIMIT = 56 * 2 ** 20

COL_TILE = 1024
N_COL_TILES = 13
PROJ_COLS = N_COL_TILES * COL_TILE
CT_GATE, CT_SGU_U, CT_SGU_V, CT_Q, CT_RKV, CT_MISC = 0, 6, 7, 8, 9, 12

IN_TM = 1024
MERGE_TM = 256
FFN_TM = 1024
FFN_TF = 512
PRE_TM = 256
POST_TM = 512
SGU_TM = 512
ROUTE_TM = 512
MOE_TM = 2048
MOE_SUB = 512
MOE_TF = 256
COMB_TM = 256


def _params(sem):
    return pltpu.CompilerParams(dimension_semantics=sem, vmem_limit_bytes=VMEM_LIMIT)


def _dot(a, b, precision=None):
    return jnp.dot(a, b, preferred_element_type=F32, precision=precision)


def _dot_nt(a, b):
    return lax.dot_general(a, b, (((1,), (1,)), ((), ())), preferred_element_type=F32)


def _dot_tn(a, b):
    return lax.dot_general(a, b, (((0,), (0,)), ((), ())), preferred_element_type=F32)


def _layer_norm(x):
    mu = jnp.mean(x, axis=-1, keepdims=True)
    xc = x - mu
    var = jnp.mean(xc * xc, axis=-1, keepdims=True)
    return xc * lax.rsqrt(var + LN_EPS)


def _gelu(x):
    return 0.5 * x * (1.0 + jnp.tanh(0.7978845608028654 * (x + 0.044715 * (x * x * x))))


def _sigmoid(x):
    return 1.0 / (1.0 + jnp.exp(-x))


def _request_of_tile(i, tm, m_prompt, t_sample):
    n_prompt_tiles = m_prompt // tm
    return jnp.where(i < n_prompt_tiles, 0, 1 + (i - n_prompt_tiles) // (t_sample // tm))


def _mod_kernel(c_ref, w_ref, b_ref, o_ref):
    c = c_ref[...]
    h = (c * _sigmoid(c)).astype(BF16)
    o_ref[0] = _dot(h, w_ref[0].astype(BF16)) + b_ref[0]


def _modulation(cond, w_mod, b_mod):
    tn = 1024
    n = w_mod.shape[-1]
    return pl.pallas_call(
        _mod_kernel,
        grid=(DEPTH, n // tn),
        in_specs=[
            pl.BlockSpec((16, D_MODEL), lambda l, j: (0, 0)),
            pl.BlockSpec((1, D_MODEL, tn), lambda l, j: (l, 0, j)),
            pl.BlockSpec((1, 1, tn), lambda l, j: (l, 0, j)),
        ],
        out_specs=pl.BlockSpec((1, 16, tn), lambda l, j: (l, 0, j)),
        out_shape=jax.ShapeDtypeStruct((DEPTH, 16, n), F32),
        compiler_params=_params(("parallel", "parallel")),
        name="modulation",
    )(cond, w_mod, b_mod.reshape(DEPTH, 1, n))


def _in_kernel(x_ref, sh_ref, sc_ref, w_ref, o_ref, h_scr):
    @pl.when(pl.program_id(1) == 0)
    def _():
        y = _layer_norm(x_ref[...])
        h_scr[...] = (y * (1.0 + sc_ref[0]) + sh_ref[0]).astype(BF16)

    o_ref[...] = _dot(h_scr[...], w_ref[0])


def _in_proj(x, mod3, w_in_p, m_prompt, t_sample):
    m = x.shape[0]
    req = functools.partial(_request_of_tile, tm=IN_TM, m_prompt=m_prompt, t_sample=t_sample)
    return pl.pallas_call(
        _in_kernel,
        grid=(m // IN_TM, N_COL_TILES),
        in_specs=[
            pl.BlockSpec((IN_TM, D_MODEL), lambda i, j: (i, 0)),
            pl.BlockSpec((1, 1, D_MODEL), lambda i, j: (req(i) * 6 + 0, 0, 0)),
            pl.BlockSpec((1, 1, D_MODEL), lambda i, j: (req(i) * 6 + 1, 0, 0)),
            pl.BlockSpec((1, D_MODEL, COL_TILE), lambda i, j: (j, 0, 0)),
        ],
        out_specs=pl.BlockSpec((IN_TM, COL_TILE), lambda i, j: (i, j)),
        out_shape=jax.ShapeDtypeStruct((m, PROJ_COLS), F32),
        scratch_shapes=[pltpu.VMEM((IN_TM, D_MODEL), BF16)],
        compiler_params=_params(("parallel", "arbitrary")),
        name="in_proj",
    )(x, mod3, mod3, w_in_p)


def _softmax_parts(scores, sink):
    m = sink
    for s in scores:
        m = jnp.maximum(m, jnp.max(s, axis=-1, keepdims=True))
    ps = [jnp.exp(s - m) for s in scores]
    den = jnp.exp(sink - m)
    for p in ps:
        den = den + jnp.sum(p, axis=-1, keepdims=True)
    inv = 1.0 / den
    return [(p * inv).astype(BF16) for p in ps]


def _gqa_attention(q, key_sets, sink_ref, o_ref):
    rows = q.shape[0]
    col = lambda kvh: slice(kvh * HEAD_DIM, (kvh + 1) * HEAD_DIM)
    heads = lambda kvh: range(kvh * A_GROUPS, (kvh + 1) * A_GROUPS)
    masks = [m for _, _, m in key_sets]
    qs = [jnp.concatenate([q[:, h * HEAD_DIM:(h + 1) * HEAD_DIM] for h in heads(kvh)], axis=0).astype(BF16)
          for kvh in range(A_KV_HEADS)]
    scores = []
    for kvh in range(A_KV_HEADS):
        per_set = []
        for (k, _, _), m in zip(key_sets, masks):
            s = _dot_nt(qs[kvh], k[:, col(kvh)]) * ATTN_SCALE
            per_set.append(s if m is None else jnp.where(m, s, NEG_INF))
        scores.append(per_set)
    probs = []
    for kvh in range(A_KV_HEADS):
        sink = jnp.concatenate([jnp.full((rows, 1), sink_ref[h], F32) for h in heads(kvh)], axis=0)
        probs.append(_softmax_parts(scores[kvh], sink))
    for kvh in range(A_KV_HEADS):
        o = None
        for (_, v, _), p in zip(key_sets, probs[kvh]):
            term = _dot(p, v[:, col(kvh)])
            o = term if o is None else o + term
        for g, h in enumerate(heads(kvh)):
            o_ref[:, h * HEAD_DIM:(h + 1) * HEAD_DIM] = o[g * rows:(g + 1) * rows].astype(BF16)


def _attn_ctx_kernel(sink_ref, q_ref, kv_ref, o_ref):
    k = kv_ref[:, :A_KV_WIDTH].astype(BF16)
    v = kv_ref[:, A_KV_WIDTH:].astype(BF16)
    _gqa_attention(q_ref[...], [(k, v, None)], sink_ref, o_ref)


def _attn_context(proj, sink, n_seq, t):
    kv_w = 2 * A_KV_WIDTH
    return pl.pallas_call(
        _attn_ctx_kernel,
        grid=(n_seq,),
        in_specs=[
            pl.BlockSpec(memory_space=pltpu.SMEM),
            pl.BlockSpec((t, A_WIDTH), lambda b: (b, CT_Q)),
            pl.BlockSpec((t, kv_w), lambda b: (b, CT_MISC * COL_TILE // kv_w)),
        ],
        out_specs=pl.BlockSpec((t, A_WIDTH), lambda b: (b, 0)),
        out_shape=jax.ShapeDtypeStruct((n_seq * t, A_WIDTH), BF16),
        compiler_params=_params(("parallel",)),
        name="attn_context",
    )(sink, proj, proj)


def _rope(x, cos, sin_signed):
    w = x.shape[-1]
    lane = lax.broadcasted_iota(jnp.int32, x.shape, 1)
    swapped = jnp.where((lane % 32) < 16, pltpu.roll(x, w - 16, 1), pltpu.roll(x, 16, 1))
    return x * cos + swapped * sin_signed


def _attn_lat_kernel(sink_ref, q_ref, kv_ref, cq_ref, sq_ref, ck_ref, sk_ref, kc_ref, vc_ref, o_ref,
                     k_scr, v_scr, kc_scr, vc_scr, *, t):
    n = pl.program_id(1)
    span = ATTN_BLOCK + 2 * WINDOW

    @pl.when(n == 0)
    def _():
        k_scr[...] = _rope(kv_ref[:, :A_KV_WIDTH], ck_ref[...], sk_ref[...]).astype(BF16)
        v_scr[...] = kv_ref[:, A_KV_WIDTH:].astype(BF16)
        kc_scr[...] = kc_ref[0, 0].astype(BF16)
        vc_scr[...] = vc_ref[0, 0].astype(BF16)

    q = _rope(q_ref[...], cq_ref[...], sq_ref[...])
    start = pl.multiple_of(jnp.clip(n * ATTN_BLOCK - WINDOW, 0, t - span), ATTN_BLOCK)
    k_loc = k_scr[pl.ds(start, span), :]
    v_loc = v_scr[pl.ds(start, span), :]
    stacked = (A_GROUPS * ATTN_BLOCK, span)
    q_pos = n * ATTN_BLOCK + lax.broadcasted_iota(jnp.int32, stacked, 0) % ATTN_BLOCK
    k_pos = start + lax.broadcasted_iota(jnp.int32, stacked, 1)
    valid = jnp.logical_and(q_pos - k_pos <= WINDOW, k_pos - q_pos <= WINDOW)
    _gqa_attention(q, [(k_loc, v_loc, valid), (kc_scr[...], vc_scr[...], None)], sink_ref, o_ref)


def _attn_latent(proj, sink, cache_k, cache_v, layer, cos, sin, m_prompt, n_seq, t):
    kv_w = 2 * A_KV_WIDTH
    nb = t // ATTN_BLOCK
    past = cache_k.shape[2]
    row0 = m_prompt // ATTN_BLOCK
    seq0 = m_prompt // t
    return pl.pallas_call(
        functools.partial(_attn_lat_kernel, t=t),
        grid=(n_seq, nb),
        in_specs=[
            pl.BlockSpec(memory_space=pltpu.SMEM),
            pl.BlockSpec((ATTN_BLOCK, A_WIDTH), lambda b, n: (row0 + b * nb + n, CT_Q)),
            pl.BlockSpec((t, kv_w), lambda b, n: (seq0 + b, CT_MISC * COL_TILE // kv_w)),
            pl.BlockSpec((ATTN_BLOCK, A_WIDTH), lambda b, n: (n, 0)),
            pl.BlockSpec((ATTN_BLOCK, A_WIDTH), lambda b, n: (n, 0)),
            pl.BlockSpec((t, A_KV_WIDTH), lambda b, n: (0, 0)),
            pl.BlockSpec((t, A_KV_WIDTH), lambda b, n: (0, 0)),
            pl.BlockSpec((1, 1, past, A_KV_WIDTH), lambda b, n: (b, layer, 0, 0)),
            pl.BlockSpec((1, 1, past, A_KV_WIDTH), lambda b, n: (b, layer, 0, 0)),
        ],
        out_specs=pl.BlockSpec((ATTN_BLOCK, A_WIDTH), lambda b, n: (b * nb + n, 0)),
        out_shape=jax.ShapeDtypeStruct((n_seq * t, A_WIDTH), BF16),
        scratch_shapes=[
            pltpu.VMEM((t, A_KV_WIDTH), BF16),
            pltpu.VMEM((t, A_KV_WIDTH), BF16),
            pltpu.VMEM((past, A_KV_WIDTH), BF16),
            pltpu.VMEM((past, A_KV_WIDTH), BF16),
        ],
        compiler_params=_params(("parallel", "arbitrary")),
        name="attn_latent",
    )(sink, proj, proj, cos, sin, cos, sin, cache_k, cache_v)


def _rope_tables(t):
    pos = jnp.arange(t)
    half = HEAD_DIM // 2
    freqs = ROPE_BASE ** (-jnp.arange(0, half, 2, dtype=F32) / half)

    def tab(p):
        ang = p.astype(F32)[:, None] * freqs[None, :]
        c, s = jnp.cos(ang), jnp.sin(ang)
        return jnp.concatenate([c, c], -1), jnp.concatenate([-s, s], -1)

    c_row, s_row = tab(pos // GRID_W)
    c_col, s_col = tab(pos % GRID_W)
    cos = jnp.concatenate([c_row, c_col], -1)
    sin = jnp.concatenate([s_row, s_col], -1)
    return jnp.tile(cos, (1, A_HEADS)), jnp.tile(sin, (1, A_HEADS))


def _sgu_kernel(u_ref, v_ref, ng_ref, w_ref, b_ref, o_ref):
    for ch in range(SGU_TM // SGU_CHUNK):
        rs = slice(ch * SGU_CHUNK, (ch + 1) * SGU_CHUNK)
        for g in range(SGU_GROUPS):
            cs = slice(g * LANES, (g + 1) * LANES)
            vn = _layer_norm(_gelu(v_ref[rs, cs])) * ng_ref[:, cs]
            mixed = _dot(w_ref[g], vn.astype(BF16)) + b_ref[g]
            o_ref[rs, cs] = (_gelu(u_ref[rs, cs]) * mixed).astype(BF16)


def _sgu(proj, norm_g, w_s, bias):
    m = proj.shape[0]
    return pl.pallas_call(
        _sgu_kernel,
        grid=(m // SGU_TM,),
        in_specs=[
            pl.BlockSpec((SGU_TM, SGU_WIDTH), lambda i: (i, CT_SGU_U)),
            pl.BlockSpec((SGU_TM, SGU_WIDTH), lambda i: (i, CT_SGU_V)),
            pl.BlockSpec((1, SGU_WIDTH), lambda i: (0, 0)),
            pl.BlockSpec((SGU_GROUPS, SGU_CHUNK, SGU_CHUNK), lambda i: (0, 0, 0)),
            pl.BlockSpec((SGU_GROUPS, SGU_CHUNK, LANES), lambda i: (0, 0, 0)),
        ],
        out_specs=pl.BlockSpec((SGU_TM, SGU_WIDTH), lambda i: (i, 0)),
        out_shape=jax.ShapeDtypeStruct((m, SGU_WIDTH), BF16),
        compiler_params=_params(("parallel",)),
        name="sgu",
    )(proj, proj, norm_g, w_s, bias)


def _head_sums(x, ones_bd):
    return _dot(x, ones_bd, precision=HIGHEST)


def _rwkv_pre_kernel(x_ref, xp_ref, xn_ref, lo_ref, lop_ref, lon_ref, mu_ref, mul_ref, kk_ref, ka_ref,
                     rk_ref, w0_ref, a0_ref, w2_ref, a2_ref, g2_ref, ones_ref,
                     r_ref, v_ref, kkn_ref, gate_ref, bonus_ref,
                     lw0_ref, b0_ref, k0_ref, lw1_ref, b1_ref, k1_ref, *, n_prompt_tiles, tiles_per_seq):
    i = pl.program_id(0)
    is_prompt = i < n_prompt_tiles
    j = lax.rem(jnp.maximum(i - n_prompt_tiles, 0), tiles_per_seq)
    first = jnp.logical_or(is_prompt, j == 0)
    last = jnp.logical_or(is_prompt, j == tiles_per_seq - 1)

    def shift(x, xp, xn, mu):
        tm = x.shape[0]
        row = lax.broadcasted_iota(jnp.int32, x.shape, 0)
        prev_row = jnp.where(first, 0.0, xp[7:8, :])
        next_row = jnp.where(last, 0.0, xn[0:1, :])
        prev = jnp.where(row == 0, prev_row, pltpu.roll(x, 1, 0))
        nxt = jnp.where(row == tm - 1, next_row, pltpu.roll(x, tm - 1, 0))
        return x + mu[0:1, :] * (prev - x) + mu[1:2, :] * (nxt - x)

    xs = shift(x_ref[...], xp_ref[...], xn_ref[...], mu_ref[...])
    lo = shift(lo_ref[...], lop_ref[...], lon_ref[...], mul_ref[...])
    r = xs[:, :C_WIDTH]
    k = xs[:, C_WIDTH:2 * C_WIDTH]
    v = xs[:, 2 * C_WIDTH:]
    tw = jnp.tanh(lo[:, :W_LORA]).astype(BF16)
    al = lo[:, W_LORA:W_LORA + A_LORA].astype(BF16)
    gl = _sigmoid(lo[:, W_LORA + A_LORA:]).astype(BF16)
    ones_bd = ones_ref[...]

    r_ref[...] = r
    v_ref[...] = v
    gate_ref[...] = _dot(gl, g2_ref[...])
    kk0 = k * kk_ref[...]
    for p in range(HEAD_PAIRS):
        cs = slice(p * LANES, (p + 1) * LANES)
        ss = _head_sums(kk0[:, cs] * kk0[:, cs], ones_bd)
        kkn_ref[:, cs] = kk0[:, cs] * lax.rsqrt(jnp.maximum(ss, 1e-24))
    kkn = kkn_ref[...]

    bonus = None
    for d, (lw_ref, b_ref, kd_ref) in enumerate(((lw0_ref, b0_ref, k0_ref), (lw1_ref, b1_ref, k1_ref))):
        z = w0_ref[d:d + 1, :] + _dot(tw, w2_ref[d])
        softplus_neg = jnp.maximum(-z, 0.0) + jnp.log(1.0 + jnp.exp(-jnp.abs(z)))
        lw_ref[...] = -jnp.exp(-softplus_neg - 0.5)
        a = _sigmoid(a0_ref[d:d + 1, :] + _dot(al, a2_ref[d]))
        kd = k * (1.0 + (a - 1.0) * ka_ref[...])
        kd_ref[...] = kd
        b_ref[...] = kkn * a
        rkr = r * kd * rk_ref[...]
        parts = [_head_sums(rkr[:, p * LANES:(p + 1) * LANES], ones_bd) for p in range(HEAD_PAIRS)]
        term = jnp.concatenate(parts, axis=1) * v
        bonus = term if bonus is None else bonus + term
    bonus_ref[...] = bonus


def _rwkv_pre(proj, lp, ones_bd, m_prompt, t_sample):
    m = proj.shape[0]
    tm = PRE_TM
    rkv_w = 3 * C_WIDTH
    lo_blk = (CT_MISC * COL_TILE + 2 * A_KV_WIDTH) // LORA_COLS
    n8 = m // 8
    prev8 = lambda i: jnp.maximum(i * (tm // 8) - 1, 0)
    next8 = lambda i: jnp.minimum((i + 1) * (tm // 8), n8 - 1)
    full = lambda shape: pl.BlockSpec(shape, lambda i: (0,) * len(shape))
    out_spec = pl.BlockSpec((tm, C_WIDTH), lambda i: (i, 0))
    out_shape = jax.ShapeDtypeStruct((m, C_WIDTH), F32)
    kern = functools.partial(_rwkv_pre_kernel, n_prompt_tiles=m_prompt // tm, tiles_per_seq=t_sample // tm)
    return pl.pallas_call(
        kern,
        grid=(m // tm,),
        in_specs=[
            pl.BlockSpec((tm, rkv_w), lambda i: (i, CT_RKV * COL_TILE // rkv_w)),
            pl.BlockSpec((8, rkv_w), lambda i: (prev8(i), CT_RKV * COL_TILE // rkv_w)),
            pl.BlockSpec((8, rkv_w), lambda i: (next8(i), CT_RKV * COL_TILE // rkv_w)),
            pl.BlockSpec((tm, LORA_COLS), lambda i: (i, lo_blk)),
            pl.BlockSpec((8, LORA_COLS), lambda i: (prev8(i), lo_blk)),
            pl.BlockSpec((8, LORA_COLS), lambda i: (next8(i), lo_blk)),
            full((2, rkv_w)), full((2, LORA_COLS)), full((1, C_WIDTH)), full((1, C_WIDTH)),
            full((1, C_WIDTH)), full((2, C_WIDTH)), full((2, C_WIDTH)),
            full((2, W_LORA, C_WIDTH)), full((2, A_LORA, C_WIDTH)), full((G_LORA, C_WIDTH)),
            full((LANES, LANES)),
        ],
        out_specs=[out_spec] * 11,
        out_shape=[out_shape] * 11,
        compiler_params=_params(("parallel",)),
        name="rwkv_pre",
    )(proj, proj, proj, proj, proj, proj, lp["mu_rkv"], lp["mu_lora"], lp["k_k"], lp["k_a"], lp["r_k"],
      lp["w0"], lp["a0"], lp["w2"], lp["a2"], lp["g2"], ones_bd)


def _expand_heads(x, lane_lo):
    return jnp.concatenate([jnp.where(lane_lo, x, 0.0), jnp.where(lane_lo, 0.0, x)], axis=0).astype(BF16)


def _scan_masks(reverse):
    c = SCAN_CHUNK
    ri = lax.broadcasted_iota(jnp.int32, (2 * c, 2 * c), 0)
    ci = lax.broadcasted_iota(jnp.int32, (2 * c, 2 * c), 1)
    same = (ri // c) == (ci // c)
    tr, tc = ri % c, ci % c
    strict = jnp.logical_and(same, (tc > tr) if reverse else (tc < tr))
    incl = jnp.logical_and(same, (tc >= tr) if reverse else (tc <= tr))
    levels = []
    s = 1
    while s < c:
        late, early = (ci, ri) if reverse else (ri, ci)
        blk = jnp.logical_and((ri // (2 * s)) == (ci // (2 * s)),
                              jnp.logical_and(late % (2 * s) >= s, early % (2 * s) < s))
        levels.append(blk)
        s *= 2
    eye = ri == ci
    return strict, incl, levels, eye


def _scan_chunks(chains):
    c = SCAN_CHUNK
    n = 2 * c
    lane_lo = lax.broadcasted_iota(jnp.int32, (c, LANES), 1) < HEAD_DIM
    n_levels = len(chains[0][8][2])

    prep = []
    for r, v, kk, lw, b, k, cum, state, masks, reverse in chains:
        cum_end = cum[0:1, :] if reverse else cum[c - 1:c, :]
        inv = jnp.exp(-cum)
        tail = jnp.exp(cum_end - cum)
        ar = jnp.concatenate([_expand_heads(kk * jnp.exp(cum - lw), lane_lo),
                              _expand_heads(r * jnp.exp(cum), lane_lo)], axis=0)
        bk = jnp.concatenate([_expand_heads(b * inv, lane_lo), _expand_heads(k * inv, lane_lo)], axis=0)
        bk_tail = jnp.concatenate([_expand_heads(b * tail, lane_lo), _expand_heads(k * tail, lane_lo)], axis=0)
        prep.append((ar, bk, bk_tail, _expand_heads(v, lane_lo), jnp.exp(cum_end)))

    gs = [_dot_nt(ar, bk) for ar, bk, _, _, _ in prep]
    xs = [_dot_nt(p[0], ch[7].astype(BF16)) for p, ch in zip(prep, chains)]
    rhs = [x[:n] + _dot(jnp.where(ch[8][0], g[:n, n:], 0.0).astype(BF16), p[3])
           for x, g, p, ch in zip(xs, gs, prep, chains)]
    t_inv = [jnp.where(ch[8][2][0], -g[:n, :n], jnp.where(ch[8][3], 1.0, 0.0)) for g, ch in zip(gs, chains)]
    for lvl in range(1, n_levels):
        t_b = [t.astype(BF16) for t in t_inv]
        half = [_dot(tb, jnp.where(ch[8][2][lvl], g[:n, :n], 0.0).astype(BF16)).astype(BF16)
                for tb, g, ch in zip(t_b, gs, chains)]
        t_inv = [t - _dot(h, tb) for t, h, tb in zip(t_inv, half, t_b)]
    us = [-_dot(t.astype(BF16), z.astype(BF16)) for t, z in zip(t_inv, rhs)]
    uvs = [jnp.concatenate([u.astype(BF16), p[3]], axis=0) for u, p in zip(us, prep)]
    out = []
    for x, g, uv, p, ch in zip(xs, gs, uvs, prep, chains):
        incl = ch[8][1]
        d_r = jnp.concatenate([jnp.where(incl, g[n:, :n], 0.0), jnp.where(incl, g[n:, n:], 0.0)], axis=1)
        ye = x[n:] + _dot(d_r.astype(BF16), uv)
        out.append(ye[:c] + ye[c:])
    new_states = [ch[7] * p[4] + _dot_tn(uv, p[2]) for uv, p, ch in zip(uvs, prep, chains)]
    return list(zip(out, new_states))


def _scan_kernel(*refs, zero_init):
    fwd, bwd, (tri_f, tri_b) = refs[0:6], refs[6:12], refs[12:14]
    if zero_init:
        yf_ref, yb_ref, sfin_ref, s_ref = refs[14:18]
    else:
        s0_ref, yf_ref, yb_ref, s_ref = refs[14:18]
    head_block = lambda h: (slice(h * HEAD_DIM, (h + 1) * HEAD_DIM),) * 2

    @pl.when(pl.program_id(1) == 0)
    def _():
        s_ref[...] = jnp.zeros(s_ref.shape, F32)
        if not zero_init:
            for d in range(2):
                for p in range(HEAD_PAIRS):
                    for h in range(2):
                        s_ref[(d, p) + head_block(h)] = s0_ref[0, d, 2 * p + h]

    chains, dests = [], []
    for d, (ins, tri, y_ref) in enumerate(((fwd, tri_f, yf_ref), (bwd, tri_b, yb_ref))):
        r_ref, v_ref, kk_ref, lw_ref, b_ref, k_ref = ins
        cum_all = _dot(tri[...], lw_ref[...], precision=HIGHEST)
        masks = _scan_masks(reverse=(d == 1))
        for p in range(HEAD_PAIRS):
            cs = slice(p * LANES, (p + 1) * LANES)
            chains.append((r_ref[:, cs], v_ref[:, cs], kk_ref[:, cs], lw_ref[:, cs], b_ref[:, cs], k_ref[:, cs],
                           cum_all[:, cs], s_ref[d, p], masks, d == 1))
            dests.append((y_ref, d, p, cs))
    for (y_ref, d, p, cs), (y, new_state) in zip(dests, _scan_chunks(chains)):
        y_ref[:, cs] = y
        s_ref[d, p] = new_state

    if zero_init:
        @pl.when(pl.program_id(1) == pl.num_programs(1) - 1)
        def _():
            for d in range(2):
                for p in range(HEAD_PAIRS):
                    for h in range(2):
                        sfin_ref[0, d, 2 * p + h] = s_ref[(d, p) + head_block(h)]


def _rwkv_scan(pre, states, layer, row0, n_seq, t):
    r, v, kk, _, _, lw0, b0, k0, lw1, b1, k1 = pre
    c = SCAN_CHUNK
    nc = t // c
    blk0 = row0 // c
    fspec = pl.BlockSpec((c, C_WIDTH), lambda s, j: (blk0 + s * nc + j, 0))
    bspec = pl.BlockSpec((c, C_WIDTH), lambda s, j: (blk0 + s * nc + nc - 1 - j, 0))
    sblock = (1, 2, C_HEADS, HEAD_DIM, HEAD_DIM)
    tspec = pl.BlockSpec((c, c), lambda s, j: (0, 0))
    tri_f = jnp.tril(jnp.ones((c, c), F32))
    args = [r, v, kk, lw0, b0, k0, r, v, kk, lw1, b1, k1, tri_f, tri_f.T]
    in_specs = [fspec] * 6 + [bspec] * 6 + [tspec, tspec]
    out_specs = [pl.BlockSpec((c, C_WIDTH), lambda s, j: (s * nc + j, 0)),
                 pl.BlockSpec((c, C_WIDTH), lambda s, j: (s * nc + nc - 1 - j, 0))]
    out_shape = [jax.ShapeDtypeStruct((n_seq * t, C_WIDTH), F32)] * 2
    if states is None:
        out_specs.append(pl.BlockSpec(sblock, lambda s, j: (s, 0, 0, 0, 0)))
        out_shape.append(jax.ShapeDtypeStruct((n_seq,) + sblock[1:], F32))
    else:
        args.append(states)
        in_specs.append(pl.BlockSpec(sblock, lambda s, j: (s, layer, 0, 0, 0)))
    return pl.pallas_call(
        functools.partial(_scan_kernel, zero_init=states is None),
        grid=(n_seq, nc),
        in_specs=in_specs,
        out_specs=out_specs,
        out_shape=out_shape,
        scratch_shapes=[pltpu.VMEM((2, HEAD_PAIRS, LANES, LANES), F32)],
        compiler_params=_params(("parallel", "arbitrary")),
        name="rwkv_scan",
    )(*args)


def _rwkv_post_kernel(yfp_ref, ybp_ref, yfs_ref, ybs_ref, bonus_ref, gate_ref, g_ref, b_ref, ones_ref, o_ref,
                      y_scr, *, n_prompt_tiles):
    i = pl.program_id(0)

    @pl.when(i < n_prompt_tiles)
    def _():
        y_scr[...] = yfp_ref[...] + ybp_ref[...]

    @pl.when(i >= n_prompt_tiles)
    def _():
        y_scr[...] = yfs_ref[...] + ybs_ref[...]

    ones_bd = ones_ref[...]
    for p in range(HEAD_PAIRS):
        cs = slice(p * LANES, (p + 1) * LANES)
        y = y_scr[:, cs]
        mu = _head_sums(y, ones_bd) * (1.0 / HEAD_DIM)
        yc = y - mu
        var = _head_sums(yc * yc, ones_bd) * (1.0 / HEAD_DIM)
        yn = yc * lax.rsqrt(var + GN_EPS) * g_ref[:, cs] + b_ref[:, cs]
        o_ref[:, cs] = ((yn + bonus_ref[:, cs]) * gate_ref[:, cs]).astype(BF16)


def _rwkv_post(y_prompt, y_sample, bonus, gate, ln_g, ln_b, ones_bd):
    m = bonus.shape[0]
    n_p = y_prompt[0].shape[0] // POST_TM
    spec = pl.BlockSpec((POST_TM, C_WIDTH), lambda i: (i, 0))
    pspec = pl.BlockSpec((POST_TM, C_WIDTH), lambda i: (jnp.minimum(i, n_p - 1), 0))
    sspec = pl.BlockSpec((POST_TM, C_WIDTH), lambda i: (jnp.maximum(i - n_p, 0), 0))
    vec = pl.BlockSpec((1, C_WIDTH), lambda i: (0, 0))
    return pl.pallas_call(
        functools.partial(_rwkv_post_kernel, n_prompt_tiles=n_p),
        grid=(m // POST_TM,),
        in_specs=[pspec, pspec, sspec, sspec, spec, spec, vec, vec, pl.BlockSpec((LANES, LANES), lambda i: (0, 0))],
        out_specs=spec,
        out_shape=jax.ShapeDtypeStruct((m, C_WIDTH), BF16),
        scratch_shapes=[pltpu.VMEM((POST_TM, C_WIDTH), F32)],
        compiler_params=_params(("parallel",)),
        name="rwkv_post",
    )(*y_prompt, *y_sample, bonus, gate, ln_g, ln_b, ones_bd)


def _merge_kernel(x_ref, attn_p_ref, attn_s_ref, sgu_ref, rw_ref, ga_ref, gb_ref, gc_ref, bg_ref, g1_ref,
                  wa_ref, wb_ref, wc_ref, wo_ref, lg_ref, lb_ref, o_ref, attn_scr, *, n_prompt_tiles):
    i = pl.program_id(0)

    @pl.when(i < n_prompt_tiles)
    def _():
        attn_scr[...] = attn_p_ref[...]

    @pl.when(i >= n_prompt_tiles)
    def _():
        attn_scr[...] = attn_s_ref[...]

    merged = None
    for br, (a_ref, w_ref, g_ref) in enumerate(((attn_scr, wa_ref, ga_ref), (sgu_ref, wb_ref, gb_ref),
                                                 (rw_ref, wc_ref, gc_ref))):
        gate = _sigmoid(g_ref[...] + bg_ref[:, br * D_MODEL:(br + 1) * D_MODEL])
        term = gate * _dot(a_ref[...], w_ref[...])
        merged = term if merged is None else merged + term
    out = _dot(merged.astype(BF16), wo_ref[...])
    y = _layer_norm(ALPHA * x_ref[...] + g1_ref[0] * out)
    o_ref[...] = y * lg_ref[...] + lb_ref[...]


def _merge(x, attn_p, attn_s, sgu, rw, proj, mod3, lp, m_prompt, t_sample):
    m = x.shape[0]
    tm = MERGE_TM
    n_p = m_prompt // tm
    req = functools.partial(_request_of_tile, tm=tm, m_prompt=m_prompt, t_sample=t_sample)
    row = lambda w: pl.BlockSpec((tm, w), lambda i: (i, 0))
    const = lambda shape: pl.BlockSpec(shape, lambda i: (0,) * len(shape), pipeline_mode=pl.Buffered(1))
    gate = lambda br: pl.BlockSpec((tm, D_MODEL), lambda i: (i, CT_GATE * COL_TILE // D_MODEL + br))
    return pl.pallas_call(
        functools.partial(_merge_kernel, n_prompt_tiles=n_p),
        grid=(m // tm,),
        in_specs=[
            row(D_MODEL),
            pl.BlockSpec((tm, A_WIDTH), lambda i: (jnp.minimum(i, n_p - 1), 0)),
            pl.BlockSpec((tm, A_WIDTH), lambda i: (jnp.maximum(i - n_p, 0), 0)),
            row(SGU_WIDTH), row(C_WIDTH), gate(0), gate(1), gate(2),
            const((1, N_BRANCH * D_MODEL)),
            pl.BlockSpec((1, 1, D_MODEL), lambda i: (req(i) * 6 + 2, 0, 0)),
            const((A_WIDTH, D_MODEL)), const((SGU_WIDTH, D_MODEL)), const((C_WIDTH, D_MODEL)),
            const((D_MODEL, D_MODEL)), const((1, D_MODEL)), const((1, D_MODEL)),
        ],
        out_specs=row(D_MODEL),
        out_shape=jax.ShapeDtypeStruct((m, D_MODEL), F32),
        scratch_shapes=[pltpu.VMEM((tm, A_WIDTH), BF16)],
        compiler_params=_params(("parallel",)),
        name="merge",
    )(x, attn_p, attn_s, sgu, rw, proj, proj, proj, lp["b_gate"], mod3, lp["w_branch_a"], lp["w_branch_b"],
      lp["w_branch_c"], lp["w_out"], lp["ln1_g"], lp["ln1_b"])


def _ffn_kernel(x_ref, sh_ref, sc_ref, g2_ref, wg_ref, wu_ref, wo_ref, lg_ref, lb_ref, o_ref, h_scr):
    f = pl.program_id(1)

    @pl.when(f == 0)
    def _():
        h_scr[...] = (_layer_norm(x_ref[...]) * (1.0 + sc_ref[0]) + sh_ref[0]).astype(BF16)
        o_ref[...] = jnp.zeros(o_ref.shape, F32)

    h = h_scr[...]
    g = _dot(h, wg_ref[0])
    u = _dot(h, wu_ref[0])
    o_ref[...] += _dot((g * _sigmoid(g) * u).astype(BF16), wo_ref[...])

    @pl.when(f == pl.num_programs(1) - 1)
    def _():
        y = _layer_norm(ALPHA * x_ref[...] + g2_ref[0] * o_ref[...])
        o_ref[...] = y * lg_ref[...] + lb_ref[...]


def _ffn_dense(x, mod3, w_in, w_out, ln_g, ln_b, m_prompt, t_sample):
    m = x.shape[0]
    tm, tf = FFN_TM, FFN_TF
    nf = D_FF // tf
    req = functools.partial(_request_of_tile, tm=tm, m_prompt=m_prompt, t_sample=t_sample)
    modspec = lambda k: pl.BlockSpec((1, 1, D_MODEL), lambda i, f: (req(i) * 6 + k, 0, 0))
    vec = pl.BlockSpec((1, D_MODEL), lambda i, f: (0, 0))
    return pl.pallas_call(
        _ffn_kernel,
        grid=(m // tm, nf),
        in_specs=[
            pl.BlockSpec((tm, D_MODEL), lambda i, f: (i, 0)),
            modspec(3), modspec(4), modspec(5),
            pl.BlockSpec((1, D_MODEL, tf), lambda i, f: (f, 0, 0)),
            pl.BlockSpec((1, D_MODEL, tf), lambda i, f: (nf + f, 0, 0)),
            pl.BlockSpec((tf, D_MODEL), lambda i, f: (f, 0)),
            vec, vec,
        ],
        out_specs=pl.BlockSpec((tm, D_MODEL), lambda i, f: (i, 0), pipeline_mode=pl.Buffered(1)),
        out_shape=jax.ShapeDtypeStruct((m, D_MODEL), F32),
        scratch_shapes=[pltpu.VMEM((tm, D_MODEL), BF16)],
        compiler_params=_params(("parallel", "arbitrary")),
        name="ffn_dense",
    )(x, mod3, mod3, mod3, w_in, w_in, w_out, ln_g, ln_b)


def _route_kernel(x_ref, sh_ref, sc_ref, wr_ref, h_ref, idx_ref, gate_ref):
    h = _layer_norm(x_ref[...]) * (1.0 + sc_ref[0]) + sh_ref[0]
    h_ref[...] = h
    logits = _dot(h, wr_ref[...], precision=HIGHEST)
    lane = lax.broadcasted_iota(jnp.int32, logits.shape, 1)
    logits = jnp.where(lane < N_EXPERTS, logits, -jnp.inf)
    lane_f = lane.astype(F32)
    m1 = jnp.max(logits, axis=-1, keepdims=True)
    i1 = jnp.min(jnp.where(logits == m1, lane_f, float(LANES)), axis=-1, keepdims=True)
    rest = jnp.where(lane_f == i1, -jnp.inf, logits)
    m2 = jnp.max(rest, axis=-1, keepdims=True)
    i2 = jnp.min(jnp.where(rest == m2, lane_f, float(LANES)), axis=-1, keepdims=True)
    e = jnp.exp(m2 - m1)
    g1 = 1.0 / (1.0 + e)
    idx_ref[...] = jnp.where(lane == 0, i1, jnp.where(lane == 1, i2, 0.0)).astype(jnp.int32)
    gate_ref[...] = jnp.where(lane == 0, g1, jnp.where(lane == 1, e * g1, 0.0))


def _route(x, mod3, router_p, m_prompt, t_sample):
    m = x.shape[0]
    tm = ROUTE_TM
    req = functools.partial(_request_of_tile, tm=tm, m_prompt=m_prompt, t_sample=t_sample)
    modspec = lambda k: pl.BlockSpec((1, 1, D_MODEL), lambda i: (req(i) * 6 + k, 0, 0))
    return pl.pallas_call(
        _route_kernel,
        grid=(m // tm,),
        in_specs=[pl.BlockSpec((tm, D_MODEL), lambda i: (i, 0)), modspec(3), modspec(4),
                  pl.BlockSpec((D_MODEL, LANES), lambda i: (0, 0))],
        out_specs=[pl.BlockSpec((tm, D_MODEL), lambda i: (i, 0)), pl.BlockSpec((tm, LANES), lambda i: (i, 0)),
                   pl.BlockSpec((tm, LANES), lambda i: (i, 0))],
        out_shape=[jax.ShapeDtypeStruct((m, D_MODEL), F32), jax.ShapeDtypeStruct((m, LANES), jnp.int32),
                   jax.ShapeDtypeStruct((m, LANES), F32)],
        compiler_params=_params(("parallel",)),
        name="route",
    )(x, mod3, mod3, router_p)


def _row_copy(src_hbm, row, dst, dst_row, sem):
    return pltpu.make_async_copy(src_hbm.at[pl.ds(row, 1)], dst.at[pl.ds(dst_row, 1)], sem)


def _moe_kernel(tile_e_ref, tile_valid_ref, row_tok_ref, h_hbm, wg_ref, wu_ref, wo_ref, o_ref, x_scr, sem):
    i = pl.program_id(0)
    f = pl.program_id(1)
    valid = tile_valid_ref[i]
    n_sub = lax.shift_right_logical(valid + (MOE_SUB - 1), MOE_SUB.bit_length() - 1)
    n_rows = n_sub * MOE_SUB

    @pl.when(f == 0)
    def _():
        def start(r2, carry):
            for prio in range(2):
                r = 2 * r2 + prio
                _row_copy(h_hbm, row_tok_ref[0, 0, r], o_ref, r, sem).start(priority=prio)
            return carry

        def wait(r, carry):
            _row_copy(h_hbm, 0, o_ref, r, sem).wait()
            return carry

        lax.fori_loop(0, n_rows // 2, start, 0)
        lax.fori_loop(0, n_rows, wait, 0)
        for nb in range(1, MOE_TM // MOE_SUB + 1):
            @pl.when(n_sub == nb)
            def _():
                x_scr[:nb * MOE_SUB, :] = o_ref[:nb * MOE_SUB, :].astype(BF16)
        o_ref[...] = jnp.zeros(o_ref.shape, F32)

    for nb in range(1, MOE_TM // MOE_SUB + 1):
        @pl.when(n_sub == nb)
        def _():
            rows = nb * MOE_SUB
            x = x_scr[:rows, :]
            g = _dot(x, wg_ref[0].astype(BF16))
            u = _dot(x, wu_ref[0].astype(BF16))
            o_ref[:rows, :] += _dot((g * _sigmoid(g) * u).astype(BF16), wo_ref[0].astype(BF16))


def _moe_experts(h, row_tok, tile_e, tile_valid, w_in, w_out):
    n_tiles = tile_e.shape[0]
    nf = D_FF_EXPERT // MOE_TF
    fidx = lambda i, f, tv: jnp.where(tv[i] > 0, f, nf - 1)
    grid_spec = pltpu.PrefetchScalarGridSpec(
        num_scalar_prefetch=2,
        grid=(n_tiles, nf),
        in_specs=[
            pl.BlockSpec((1, 1, MOE_TM), lambda i, f, te, tv: (i, 0, 0), memory_space=pltpu.SMEM),
            pl.BlockSpec(memory_space=pl.ANY),
            pl.BlockSpec((1, D_MODEL, MOE_TF), lambda i, f, te, tv: (te[i], 0, fidx(i, f, tv))),
            pl.BlockSpec((1, D_MODEL, MOE_TF), lambda i, f, te, tv: (te[i], 0, nf + fidx(i, f, tv))),
            pl.BlockSpec((1, MOE_TF, D_MODEL), lambda i, f, te, tv: (te[i], fidx(i, f, tv), 0)),
        ],
        out_specs=pl.BlockSpec((MOE_TM, D_MODEL), lambda i, f, te, tv: (i, 0), pipeline_mode=pl.Buffered(1)),
        scratch_shapes=[pltpu.VMEM((MOE_TM, D_MODEL), BF16), pltpu.SemaphoreType.DMA(())],
    )
    return pl.pallas_call(
        _moe_kernel,
        grid_spec=grid_spec,
        out_shape=jax.ShapeDtypeStruct((n_tiles * MOE_TM, D_MODEL), F32),
        compiler_params=_params(("arbitrary", "arbitrary")),
        name="moe_experts",
    )(tile_e, tile_valid, row_tok.reshape(n_tiles, 1, MOE_TM), h, w_in, w_in, w_out)


def _combine_kernel(pos_ref, pos_next_ref, ys_hbm, x_ref, gate_ref, g2_ref, lg_ref, lb_ref, o_ref, buf, sems):
    tm = COMB_TM
    i = pl.program_id(0)
    slot = lax.rem(i, 2)

    def gather(p_ref, s):
        def start(t, carry):
            for k in range(TOP_K):
                _row_copy(ys_hbm, p_ref[0, 0, t * TOP_K + k], buf.at[s, k], t, sems.at[s]).start(priority=k)
            return carry
        lax.fori_loop(0, tm, start, 0)

    @pl.when(i == 0)
    def _():
        gather(pos_ref, 0)

    @pl.when(i + 1 < pl.num_programs(0))
    def _():
        gather(pos_next_ref, 1 - slot)

    def wait(t, carry):
        for k in range(TOP_K):
            _row_copy(ys_hbm, 0, buf.at[slot, k], t, sems.at[slot]).wait()
        return carry

    lax.fori_loop(0, tm, wait, 0)
    y = gate_ref[:, 0:1] * buf[slot, 0] + gate_ref[:, 1:2] * buf[slot, 1]
    z = _layer_norm(ALPHA * x_ref[...] + g2_ref[0] * y)
    o_ref[...] = z * lg_ref[...] + lb_ref[...]


def _moe_combine(x, ys, pos, gates, mod3, ln_g, ln_b, m_prompt, t_sample):
    m = x.shape[0]
    tm = COMB_TM
    n = m // tm
    req = functools.partial(_request_of_tile, tm=tm, m_prompt=m_prompt, t_sample=t_sample)
    pos3 = pos.reshape(n, 1, tm * TOP_K)
    return pl.pallas_call(
        _combine_kernel,
        grid=(n,),
        in_specs=[
            pl.BlockSpec((1, 1, tm * TOP_K), lambda i: (i, 0, 0), memory_space=pltpu.SMEM),
            pl.BlockSpec((1, 1, tm * TOP_K), lambda i: (jnp.minimum(i + 1, n - 1), 0, 0), memory_space=pltpu.SMEM),
            pl.BlockSpec(memory_space=pl.ANY),
            pl.BlockSpec((tm, D_MODEL), lambda i: (i, 0)),
            pl.BlockSpec((tm, LANES), lambda i: (i, 0)),
            pl.BlockSpec((1, 1, D_MODEL), lambda i: (req(i) * 6 + 5, 0, 0)),
            pl.BlockSpec((1, D_MODEL), lambda i: (0, 0)),
            pl.BlockSpec((1, D_MODEL), lambda i: (0, 0)),
        ],
        out_specs=pl.BlockSpec((tm, D_MODEL), lambda i: (i, 0)),
        out_shape=jax.ShapeDtypeStruct((m, D_MODEL), F32),
        scratch_shapes=[pltpu.VMEM((2, TOP_K, tm, D_MODEL), F32), pltpu.SemaphoreType.DMA((2,))],
        compiler_params=_params(("arbitrary",)),
        name="moe_combine",
    )(pos3, pos3, ys, x, gates, mod3, ln_g, ln_b)


def _moe_plan(idx):
    m = idx.shape[0]
    flat_e = idx.reshape(-1)
    n_assign = m * TOP_K
    onehot = (flat_e[:, None] == jnp.arange(N_EXPERTS, dtype=jnp.int32)[None, :]).astype(jnp.int32)
    csum = jnp.cumsum(onehot, axis=0)
    counts = csum[-1]
    rank = jnp.take_along_axis(csum, flat_e[:, None], axis=1)[:, 0] - 1
    padded = (counts + MOE_TM - 1) // MOE_TM * MOE_TM
    pad_end = jnp.cumsum(padded)
    pad_start = pad_end - padded
    pos = (pad_start[flat_e] + rank).astype(jnp.int32)
    n_tiles = n_assign // MOE_TM + N_EXPERTS
    row_tok = jnp.zeros((n_tiles * MOE_TM,), jnp.int32).at[pos].set(jnp.arange(n_assign, dtype=jnp.int32) // TOP_K)
    tile_start = jnp.arange(n_tiles, dtype=jnp.int32) * MOE_TM
    tile_e = jnp.sum((pad_end[None, :] <= tile_start[:, None]).astype(jnp.int32), axis=1)
    used = tile_e < N_EXPERTS
    last_e = jnp.max(jnp.where(counts > 0, jnp.arange(N_EXPERTS, dtype=jnp.int32), 0))
    tile_e = jnp.where(used, tile_e, last_e)
    seg_end = pad_start[tile_e] + counts[tile_e]
    tile_valid = jnp.where(used, jnp.clip(seg_end - tile_start, 0, MOE_TM), 0).astype(jnp.int32)
    return row_tok, tile_e, tile_valid, pos


def _column_tiles(w, tile):
    k, n = w.shape[-2:]
    w = w.reshape(w.shape[:-1] + (n // tile, tile))
    return jnp.moveaxis(w, -2, -3)


def _reorder_w_in(w_in):
    o_q, o_k, o_v = 0, A_WIDTH, A_WIDTH + A_KV_WIDTH
    o_xb = o_v + A_KV_WIDTH
    o_xc = o_xb + 2 * SGU_WIDTH
    o_lora = o_xc + 3 * C_WIDTH
    o_gl = o_xc + C_COLS
    cols = lambda a, n: w_in[:, :, a:a + n]
    pad = jnp.zeros(w_in.shape[:2] + (COL_TILE - 2 * A_KV_WIDTH - LORA_COLS,), w_in.dtype)
    parts = [cols(o_gl, N_BRANCH * D_MODEL), cols(o_xb, 2 * SGU_WIDTH), cols(o_q, A_WIDTH),
             cols(o_xc, 3 * C_WIDTH), cols(o_k, 2 * A_KV_WIDTH), cols(o_lora, LORA_COLS), pad]
    return _column_tiles(jnp.concatenate(parts, axis=-1).astype(BF16), COL_TILE)


def kernel(x_prompt, x_sample, cache_k, cache_v, state_rwkv, c, c_ctx, w_mod, b_mod, w_in, b_gate, attn_sink, sgu_norm_g, sgu_w, sgu_b, rwkv_mu, rwkv_w0, rwkv_w2, rwkv_a0, rwkv_a2, rwkv_g2, rwkv_k_k, rwkv_k_a, rwkv_r_k, rwkv_ln_g, rwkv_ln_b, w_branch_a, w_branch_b, w_branch_c, w_out, ln1_g, ln1_b, ln2_g, ln2_b, ffn_w_in, ffn_w_out, moe_router, moe_w_in, moe_w_out):
    bp, tp, _ = x_prompt.shape
    bs, ts, _ = x_sample.shape
    past = cache_k.shape[2]
    mp, ms = bp * tp, bs * ts
    assert bs + 1 <= 16 and mp % IN_TM == 0 and ts % IN_TM == 0 and tp % PRE_TM == 0 and IN_TM % tp == 0

    x = jnp.concatenate([x_prompt.reshape(mp, D_MODEL), x_sample.reshape(ms, D_MODEL)], axis=0)
    cond = jnp.zeros((16, D_MODEL), F32).at[0].set(c_ctx).at[1:1 + bs].set(c)
    mod = _modulation(cond, w_mod, b_mod)

    w_in_p = _reorder_w_in(w_in)
    cos, sin = _rope_tables(ts)
    ones_bd = jnp.kron(jnp.eye(LANES // HEAD_DIM, dtype=F32), jnp.ones((HEAD_DIM, HEAD_DIM), F32))
    cache_k4 = cache_k.reshape(bs, DEPTH, past, A_KV_WIDTH)
    cache_v4 = cache_v.reshape(bs, DEPTH, past, A_KV_WIDTH)
    states4 = state_rwkv.astype(F32).reshape(bs, DEPTH * 2, C_HEADS, HEAD_DIM, HEAD_DIM)

    new_k, new_v, new_s = [], [], []
    for l in range(DEPTH):
        mod3 = mod[l].reshape(16 * 6, 1, D_MODEL)
        lp = {
            "mu_rkv": rwkv_mu[l][:, :3 * C_WIDTH], "mu_lora": rwkv_mu[l][:, 3 * C_WIDTH:],
            "k_k": rwkv_k_k[l].reshape(1, C_WIDTH), "k_a": rwkv_k_a[l].reshape(1, C_WIDTH),
            "r_k": rwkv_r_k[l].reshape(1, C_WIDTH), "w0": rwkv_w0[l], "a0": rwkv_a0[l],
            "w2": rwkv_w2[l].astype(BF16), "a2": rwkv_a2[l].astype(BF16), "g2": rwkv_g2[l].astype(BF16),
            "b_gate": b_gate[l].reshape(1, N_BRANCH * D_MODEL),
            "w_branch_a": w_branch_a[l].astype(BF16), "w_branch_b": w_branch_b[l].astype(BF16),
            "w_branch_c": w_branch_c[l].astype(BF16), "w_out": w_out[l].astype(BF16),
            "ln1_g": ln1_g[l].reshape(1, D_MODEL), "ln1_b": ln1_b[l].reshape(1, D_MODEL),
        }
        proj = _in_proj(x, mod3, w_in_p[l], mp, ts)

        attn_p = _attn_context(proj, attn_sink[l], bp, tp)
        attn_s = _attn_latent(proj, attn_sink[l], cache_k4, cache_v4, l, cos, sin, mp, bs, ts)
        kv_cols = proj[:mp, CT_MISC * COL_TILE:CT_MISC * COL_TILE + 2 * A_KV_WIDTH]
        new_k.append(kv_cols[:, :A_KV_WIDTH].reshape(bp, tp, A_KV_HEADS, HEAD_DIM))
        new_v.append(kv_cols[:, A_KV_WIDTH:].reshape(bp, tp, A_KV_HEADS, HEAD_DIM))

        sgu_bias = jnp.broadcast_to(sgu_b[l][:, :, None], (SGU_GROUPS, SGU_CHUNK, LANES))
        sgu = _sgu(proj, sgu_norm_g[l].reshape(1, SGU_WIDTH), sgu_w[l].astype(BF16), sgu_bias)

        pre = _rwkv_pre(proj, lp, ones_bd, mp, ts)
        yf_p, yb_p, s_fin = _rwkv_scan(pre, None, l, 0, bp, tp)
        yf_s, yb_s = _rwkv_scan(pre, states4, l, mp, bs, ts)
        rw = _rwkv_post((yf_p, yb_p), (yf_s, yb_s), pre[4], pre[3],
                        rwkv_ln_g[l].reshape(1, C_WIDTH), rwkv_ln_b[l].reshape(1, C_WIDTH), ones_bd)
        new_s.append(s_fin)

        x1 = _merge(x, attn_p, attn_s, sgu, rw, proj, mod3, lp, mp, ts)

        ln2g, ln2b = ln2_g[l].reshape(1, D_MODEL), ln2_b[l].reshape(1, D_MODEL)
        if l % 2 == 0:
            ffn_in = _column_tiles(ffn_w_in[l // 2].astype(BF16), FFN_TF)
            x = _ffn_dense(x1, mod3, ffn_in, ffn_w_out[l // 2].astype(BF16), ln2g, ln2b, mp, ts)
        else:
            router_p = jnp.zeros((D_MODEL, LANES), F32).at[:, :N_EXPERTS].set(moe_router[l // 2])
            h, idx, gates = _route(x1, mod3, router_p, mp, ts)
            row_tok, tile_e, tile_valid, pos = _moe_plan(idx[:, :TOP_K])
            ys = _moe_experts(h, row_tok, tile_e, tile_valid, moe_w_in[l // 2], moe_w_out[l // 2])
            x = _moe_combine(x1, ys, pos, gates, mod3, ln2g, ln2b, mp, ts)

    y_prompt = x[:mp].reshape(bp, tp, D_MODEL)
    y_sample = x[mp:].reshape(bs, ts, D_MODEL)
    return (y_prompt, y_sample, jnp.stack(new_k, axis=1), jnp.stack(new_v, axis=1), jnp.stack(new_s, axis=1))
```

```python
import functools

import jax
import jax.numpy as jnp
from jax import lax
from jax.experimental import pallas as pl
from jax.experimental.pallas import tpu as pltpu

F32 = jnp.float32
BF16 = jnp.bfloat16
HIGHEST = lax.Precision.HIGHEST

D_MODEL = 2048
DEPTH = 2
GRID_W = 64
A_HEADS = 16
A_KV_HEADS = 4
A_GROUPS = A_HEADS // A_KV_HEADS
HEAD_DIM = 64
A_WIDTH = A_HEADS * HEAD_DIM
A_KV_WIDTH = A_KV_HEADS * HEAD_DIM
ATTN_SCALE = HEAD_DIM ** -0.5
WINDOW = 128
ATTN_BLOCK = 128
ROPE_BASE = 10000.0
SGU_CHUNK = 128
SGU_GROUPS = 8
SGU_WIDTH = 1024
C_HEADS = 16
C_WIDTH = C_HEADS * HEAD_DIM
W_LORA = 64
A_LORA = 64
G_LORA = 128
LORA_COLS = W_LORA + A_LORA + G_LORA
C_COLS = 3 * C_WIDTH + LORA_COLS
N_BRANCH = 3
D_FF = 5632
N_EXPERTS = 8
TOP_K = 2
D_FF_EXPERT = 7168
ALPHA = (2 * DEPTH) ** 0.25
LN_EPS = 1e-6
GN_EPS = HEAD_DIM * 1e-5
NEG_INF = -1e30

LANES = 128
HEAD_PAIRS = C_HEADS * HEAD_DIM // LANES
SCAN_CHUNK = 64
VMEM_LIMIT = 56 * 2 ** 20

COL_TILE = 1024
N_COL_TILES = 13
PROJ_COLS = N_COL_TILES * COL_TILE
CT_GATE, CT_SGU_U, CT_SGU_V, CT_Q, CT_RKV, CT_MISC = 0, 6, 7, 8, 9, 12

IN_TM = 1024
MERGE_TM = 256
FFN_TM = 1024
FFN_TF = 512
PRE_TM = 256
POST_TM = 512
SGU_TM = 512
ROUTE_TM = 512
MOE_TM = 2048
MOE_SUB = 512
MOE_TF = 256
COMB_TM = 256


def _params(sem):
    return pltpu.CompilerParams(dimension_semantics=sem, vmem_limit_bytes=VMEM_LIMIT)


def _dot(a, b, precision=None):
    return jnp.dot(a, b, preferred_element_type=F32, precision=precision)


def _dot_nt(a, b):
    return lax.dot_general(a, b, (((1,), (1,)), ((), ())), preferred_element_type=F32)


def _dot_tn(a, b):
    return lax.dot_general(a, b, (((0,), (0,)), ((), ())), preferred_element_type=F32)


def _layer_norm(x):
    mu = jnp.mean(x, axis=-1, keepdims=True)
    xc = x - mu
    var = jnp.mean(xc * xc, axis=-1, keepdims=True)
    return xc * lax.rsqrt(var + LN_EPS)


def _gelu(x):
    return 0.5 * x * (1.0 + jnp.tanh(0.7978845608028654 * (x + 0.044715 * (x * x * x))))


def _sigmoid(x):
    return 1.0 / (1.0 + jnp.exp(-x))


def _request_of_tile(i, tm, m_prompt, t_sample):
    n_prompt_tiles = m_prompt // tm
    return jnp.where(i < n_prompt_tiles, 0, 1 + (i - n_prompt_tiles) // (t_sample // tm))


def _mod_kernel(c_ref, w_ref, b_ref, o_ref):
    c = c_ref[...]
    h = (c * _sigmoid(c)).astype(BF16)
    o_ref[0] = _dot(h, w_ref[0].astype(BF16)) + b_ref[0]


def _modulation(cond, w_mod, b_mod):
    tn = 1024
    n = w_mod.shape[-1]
    return pl.pallas_call(
        _mod_kernel,
        grid=(DEPTH, n // tn),
        in_specs=[
            pl.BlockSpec((16, D_MODEL), lambda l, j: (0, 0)),
            pl.BlockSpec((1, D_MODEL, tn), lambda l, j: (l, 0, j)),
            pl.BlockSpec((1, 1, tn), lambda l, j: (l, 0, j)),
        ],
        out_specs=pl.BlockSpec((1, 16, tn), lambda l, j: (l, 0, j)),
        out_shape=jax.ShapeDtypeStruct((DEPTH, 16, n), F32),
        compiler_params=_params(("parallel", "parallel")),
        name="modulation",
    )(cond, w_mod, b_mod.reshape(DEPTH, 1, n))


def _in_kernel(x_ref, sh_ref, sc_ref, w_ref, o_ref, h_scr):
    @pl.when(pl.program_id(1) == 0)
    def _():
        y = _layer_norm(x_ref[...])
        h_scr[...] = (y * (1.0 + sc_ref[0]) + sh_ref[0]).astype(BF16)

    o_ref[...] = _dot(h_scr[...], w_ref[...])


def _in_proj(x, mod3, w_in_p, m_prompt, t_sample):
    m = x.shape[0]
    req = functools.partial(_request_of_tile, tm=IN_TM, m_prompt=m_prompt, t_sample=t_sample)
    return pl.pallas_call(
        _in_kernel,
        grid=(m // IN_TM, N_COL_TILES),
        in_specs=[
            pl.BlockSpec((IN_TM, D_MODEL), lambda i, j: (i, 0)),
            pl.BlockSpec((1, 1, D_MODEL), lambda i, j: (req(i) * 6 + 0, 0, 0)),
            pl.BlockSpec((1, 1, D_MODEL), lambda i, j: (req(i) * 6 + 1, 0, 0)),
            pl.BlockSpec((D_MODEL, COL_TILE), lambda i, j: (0, j)),
        ],
        out_specs=pl.BlockSpec((IN_TM, COL_TILE), lambda i, j: (i, j)),
        out_shape=jax.ShapeDtypeStruct((m, PROJ_COLS), F32),
        scratch_shapes=[pltpu.VMEM((IN_TM, D_MODEL), BF16)],
        compiler_params=_params(("parallel", "arbitrary")),
        name="in_proj",
    )(x, mod3, mod3, w_in_p)


def _softmax_parts(scores, sink):
    m = sink
    for s in scores:
        m = jnp.maximum(m, jnp.max(s, axis=-1, keepdims=True))
    ps = [jnp.exp(s - m) for s in scores]
    den = jnp.exp(sink - m)
    for p in ps:
        den = den + jnp.sum(p, axis=-1, keepdims=True)
    inv = 1.0 / den
    return [(p * inv).astype(BF16) for p in ps]


def _gqa_attention(q, key_sets, sink_ref, o_ref):
    rows = q.shape[0]
    col = lambda kvh: slice(kvh * HEAD_DIM, (kvh + 1) * HEAD_DIM)
    heads = lambda kvh: range(kvh * A_GROUPS, (kvh + 1) * A_GROUPS)
    masks = [m for _, _, m in key_sets]
    qs = [jnp.concatenate([q[:, h * HEAD_DIM:(h + 1) * HEAD_DIM] for h in heads(kvh)], axis=0).astype(BF16)
          for kvh in range(A_KV_HEADS)]
    scores = []
    for kvh in range(A_KV_HEADS):
        per_set = []
        for (k, _, _), m in zip(key_sets, masks):
            s = _dot_nt(qs[kvh], k[:, col(kvh)]) * ATTN_SCALE
            per_set.append(s if m is None else jnp.where(m, s, NEG_INF))
        scores.append(per_set)
    probs = []
    for kvh in range(A_KV_HEADS):
        sink = jnp.concatenate([jnp.full((rows, 1), sink_ref[h], F32) for h in heads(kvh)], axis=0)
        probs.append(_softmax_parts(scores[kvh], sink))
    for kvh in range(A_KV_HEADS):
        o = None
        for (_, v, _), p in zip(key_sets, probs[kvh]):
            term = _dot(p, v[:, col(kvh)])
            o = term if o is None else o + term
        for g, h in enumerate(heads(kvh)):
            o_ref[:, h * HEAD_DIM:(h + 1) * HEAD_DIM] = o[g * rows:(g + 1) * rows].astype(BF16)


def _attn_ctx_kernel(sink_ref, q_ref, kv_ref, o_ref):
    k = kv_ref[:, :A_KV_WIDTH].astype(BF16)
    v = kv_ref[:, A_KV_WIDTH:].astype(BF16)
    _gqa_attention(q_ref[...], [(k, v, None)], sink_ref, o_ref)


def _attn_context(proj, sink, n_seq, t):
    kv_w = 2 * A_KV_WIDTH
    return pl.pallas_call(
        _attn_ctx_kernel,
        grid=(n_seq,),
        in_specs=[
            pl.BlockSpec(memory_space=pltpu.SMEM),
            pl.BlockSpec((t, A_WIDTH), lambda b: (b, CT_Q)),
            pl.BlockSpec((t, kv_w), lambda b: (b, CT_MISC * COL_TILE // kv_w)),
        ],
        out_specs=pl.BlockSpec((t, A_WIDTH), lambda b: (b, 0)),
        out_shape=jax.ShapeDtypeStruct((n_seq * t, A_WIDTH), BF16),
        compiler_params=_params(("parallel",)),
        name="attn_context",
    )(sink, proj, proj)


def _rope(x, cos, sin_signed):
    w = x.shape[-1]
    lane = lax.broadcasted_iota(jnp.int32, x.shape, 1)
    swapped = jnp.where((lane % 32) < 16, pltpu.roll(x, w - 16, 1), pltpu.roll(x, 16, 1))
    return x * cos + swapped * sin_signed


def _attn_lat_kernel(sink_ref, q_ref, kv_ref, cq_ref, sq_ref, ck_ref, sk_ref, kc_ref, vc_ref, o_ref,
                     k_scr, v_scr, kc_scr, vc_scr, *, t):
    n = pl.program_id(1)
    span = ATTN_BLOCK + 2 * WINDOW

    @pl.when(n == 0)
    def _():
        k_scr[...] = _rope(kv_ref[:, :A_KV_WIDTH], ck_ref[...], sk_ref[...]).astype(BF16)
        v_scr[...] = kv_ref[:, A_KV_WIDTH:].astype(BF16)
        kc_scr[...] = kc_ref[0, 0].astype(BF16)
        vc_scr[...] = vc_ref[0, 0].astype(BF16)

    q = _rope(q_ref[...], cq_ref[...], sq_ref[...])
    start = pl.multiple_of(jnp.clip(n * ATTN_BLOCK - WINDOW, 0, t - span), ATTN_BLOCK)
    k_loc = k_scr[pl.ds(start, span), :]
    v_loc = v_scr[pl.ds(start, span), :]
    stacked = (A_GROUPS * ATTN_BLOCK, span)
    q_pos = n * ATTN_BLOCK + lax.broadcasted_iota(jnp.int32, stacked, 0) % ATTN_BLOCK
    k_pos = start + lax.broadcasted_iota(jnp.int32, stacked, 1)
    valid = jnp.logical_and(q_pos - k_pos <= WINDOW, k_pos - q_pos <= WINDOW)
    _gqa_attention(q, [(k_loc, v_loc, valid), (kc_scr[...], vc_scr[...], None)], sink_ref, o_ref)


def _attn_latent(proj, sink, cache_k, cache_v, layer, cos, sin, m_prompt, n_seq, t):
    kv_w = 2 * A_KV_WIDTH
    nb = t // ATTN_BLOCK
    past = cache_k.shape[2]
    row0 = m_prompt // ATTN_BLOCK
    seq0 = m_prompt // t
    return pl.pallas_call(
        functools.partial(_attn_lat_kernel, t=t),
        grid=(n_seq, nb),
        in_specs=[
            pl.BlockSpec(memory_space=pltpu.SMEM),
            pl.BlockSpec((ATTN_BLOCK, A_WIDTH), lambda b, n: (row0 + b * nb + n, CT_Q)),
            pl.BlockSpec((t, kv_w), lambda b, n: (seq0 + b, CT_MISC * COL_TILE // kv_w)),
            pl.BlockSpec((ATTN_BLOCK, A_WIDTH), lambda b, n: (n, 0)),
            pl.BlockSpec((ATTN_BLOCK, A_WIDTH), lambda b, n: (n, 0)),
            pl.BlockSpec((t, A_KV_WIDTH), lambda b, n: (0, 0)),
            pl.BlockSpec((t, A_KV_WIDTH), lambda b, n: (0, 0)),
            pl.BlockSpec((1, 1, past, A_KV_WIDTH), lambda b, n: (b, layer, 0, 0)),
            pl.BlockSpec((1, 1, past, A_KV_WIDTH), lambda b, n: (b, layer, 0, 0)),
        ],
        out_specs=pl.BlockSpec((ATTN_BLOCK, A_WIDTH), lambda b, n: (b * nb + n, 0)),
        out_shape=jax.ShapeDtypeStruct((n_seq * t, A_WIDTH), BF16),
        scratch_shapes=[
            pltpu.VMEM((t, A_KV_WIDTH), BF16),
            pltpu.VMEM((t, A_KV_WIDTH), BF16),
            pltpu.VMEM((past, A_KV_WIDTH), BF16),
            pltpu.VMEM((past, A_KV_WIDTH), BF16),
        ],
        compiler_params=_params(("parallel", "arbitrary")),
        name="attn_latent",
    )(sink, proj, proj, cos, sin, cos, sin, cache_k, cache_v)


def _rope_tables(t):
    pos = jnp.arange(t)
    half = HEAD_DIM // 2
    freqs = ROPE_BASE ** (-jnp.arange(0, half, 2, dtype=F32) / half)

    def tab(p):
        ang = p.astype(F32)[:, None] * freqs[None, :]
        c, s = jnp.cos(ang), jnp.sin(ang)
        return jnp.concatenate([c, c], -1), jnp.concatenate([-s, s], -1)

    c_row, s_row = tab(pos // GRID_W)
    c_col, s_col = tab(pos % GRID_W)
    cos = jnp.concatenate([c_row, c_col], -1)
    sin = jnp.concatenate([s_row, s_col], -1)
    return jnp.tile(cos, (1, A_HEADS)), jnp.tile(sin, (1, A_HEADS))


def _sgu_kernel(u_ref, v_ref, ng_ref, w_ref, b_ref, o_ref):
    for ch in range(SGU_TM // SGU_CHUNK):
        rs = slice(ch * SGU_CHUNK, (ch + 1) * SGU_CHUNK)
        for g in range(SGU_GROUPS):
            cs = slice(g * LANES, (g + 1) * LANES)
            vn = _layer_norm(_gelu(v_ref[rs, cs])) * ng_ref[:, cs]
            mixed = _dot(w_ref[g], vn.astype(BF16)) + b_ref[g]
            o_ref[rs, cs] = (_gelu(u_ref[rs, cs]) * mixed).astype(BF16)


def _sgu(proj, norm_g, w_s, bias):
    m = proj.shape[0]
    return pl.pallas_call(
        _sgu_kernel,
        grid=(m // SGU_TM,),
        in_specs=[
            pl.BlockSpec((SGU_TM, SGU_WIDTH), lambda i: (i, CT_SGU_U)),
            pl.BlockSpec((SGU_TM, SGU_WIDTH), lambda i: (i, CT_SGU_V)),
            pl.BlockSpec((1, SGU_WIDTH), lambda i: (0, 0)),
            pl.BlockSpec((SGU_GROUPS, SGU_CHUNK, SGU_CHUNK), lambda i: (0, 0, 0)),
            pl.BlockSpec((SGU_GROUPS, SGU_CHUNK, LANES), lambda i: (0, 0, 0)),
        ],
        out_specs=pl.BlockSpec((SGU_TM, SGU_WIDTH), lambda i: (i, 0)),
        out_shape=jax.ShapeDtypeStruct((m, SGU_WIDTH), BF16),
        compiler_params=_params(("parallel",)),
        name="sgu",
    )(proj, proj, norm_g, w_s, bias)


def _head_sums(x, ones_bd):
    return _dot(x, ones_bd, precision=HIGHEST)


def _rwkv_pre_kernel(x_ref, xp_ref, xn_ref, lo_ref, lop_ref, lon_ref, mu_ref, mul_ref, kk_ref, ka_ref,
                     rk_ref, w0_ref, a0_ref, w2_ref, a2_ref, g2_ref, ones_ref,
                     r_ref, v_ref, kkn_ref, gate_ref, bonus_ref,
                     lw0_ref, b0_ref, k0_ref, lw1_ref, b1_ref, k1_ref, *, n_prompt_tiles, tiles_per_seq):
    i = pl.program_id(0)
    is_prompt = i < n_prompt_tiles
    j = lax.rem(jnp.maximum(i - n_prompt_tiles, 0), tiles_per_seq)
    first = jnp.logical_or(is_prompt, j == 0)
    last = jnp.logical_or(is_prompt, j == tiles_per_seq - 1)

    def shift(x, xp, xn, mu):
        tm = x.shape[0]
        row = lax.broadcasted_iota(jnp.int32, x.shape, 0)
        prev_row = jnp.where(first, 0.0, xp[7:8, :])
        next_row = jnp.where(last, 0.0, xn[0:1, :])
        prev = jnp.where(row == 0, prev_row, pltpu.roll(x, 1, 0))
        nxt = jnp.where(row == tm - 1, next_row, pltpu.roll(x, tm - 1, 0))
        return x + mu[0:1, :] * (prev - x) + mu[1:2, :] * (nxt - x)

    xs = shift(x_ref[...], xp_ref[...], xn_ref[...], mu_ref[...])
    lo = shift(lo_ref[...], lop_ref[...], lon_ref[...], mul_ref[...])
    r = xs[:, :C_WIDTH]
    k = xs[:, C_WIDTH:2 * C_WIDTH]
    v = xs[:, 2 * C_WIDTH:]
    tw = jnp.tanh(lo[:, :W_LORA]).astype(BF16)
    al = lo[:, W_LORA:W_LORA + A_LORA].astype(BF16)
    gl = _sigmoid(lo[:, W_LORA + A_LORA:]).astype(BF16)
    ones_bd = ones_ref[...]

    r_ref[...] = r
    v_ref[...] = v
    gate_ref[...] = _dot(gl, g2_ref[...])
    kk0 = k * kk_ref[...]
    for p in range(HEAD_PAIRS):
        cs = slice(p * LANES, (p + 1) * LANES)
        ss = _head_sums(kk0[:, cs] * kk0[:, cs], ones_bd)
        kkn_ref[:, cs] = kk0[:, cs] * lax.rsqrt(jnp.maximum(ss, 1e-24))
    kkn = kkn_ref[...]

    bonus = None
    for d, (lw_ref, b_ref, kd_ref) in enumerate(((lw0_ref, b0_ref, k0_ref), (lw1_ref, b1_ref, k1_ref))):
        z = w0_ref[d:d + 1, :] + _dot(tw, w2_ref[d])
        softplus_neg = jnp.maximum(-z, 0.0) + jnp.log(1.0 + jnp.exp(-jnp.abs(z)))
        lw_ref[...] = -jnp.exp(-softplus_neg - 0.5)
        a = _sigmoid(a0_ref[d:d + 1, :] + _dot(al, a2_ref[d]))
        kd = k * (1.0 + (a - 1.0) * ka_ref[...])
        kd_ref[...] = kd
        b_ref[...] = kkn * a
        rkr = r * kd * rk_ref[...]
        parts = [_head_sums(rkr[:, p * LANES:(p + 1) * LANES], ones_bd) for p in range(HEAD_PAIRS)]
        term = jnp.concatenate(parts, axis=1) * v
        bonus = term if bonus is None else bonus + term
    bonus_ref[...] = bonus


def _rwkv_pre(proj, lp, ones_bd, m_prompt, t_sample):
    m = proj.shape[0]
    tm = PRE_TM
    rkv_w = 3 * C_WIDTH
    lo_blk = (CT_MISC * COL_TILE + 2 * A_KV_WIDTH) // LORA_COLS
    n8 = m // 8
    prev8 = lambda i: jnp.maximum(i * (tm // 8) - 1, 0)
    next8 = lambda i: jnp.minimum((i + 1) * (tm // 8), n8 - 1)
    full = lambda shape: pl.BlockSpec(shape, lambda i: (0,) * len(shape))
    out_spec = pl.BlockSpec((tm, C_WIDTH), lambda i: (i, 0))
    out_shape = jax.ShapeDtypeStruct((m, C_WIDTH), F32)
    kern = functools.partial(_rwkv_pre_kernel, n_prompt_tiles=m_prompt // tm, tiles_per_seq=t_sample // tm)
    return pl.pallas_call(
        kern,
        grid=(m // tm,),
        in_specs=[
            pl.BlockSpec((tm, rkv_w), lambda i: (i, CT_RKV * COL_TILE // rkv_w)),
            pl.BlockSpec((8, rkv_w), lambda i: (prev8(i), CT_RKV * COL_TILE // rkv_w)),
            pl.BlockSpec((8, rkv_w), lambda i: (next8(i), CT_RKV * COL_TILE // rkv_w)),
            pl.BlockSpec((tm, LORA_COLS), lambda i: (i, lo_blk)),
            pl.BlockSpec((8, LORA_COLS), lambda i: (prev8(i), lo_blk)),
            pl.BlockSpec((8, LORA_COLS), lambda i: (next8(i), lo_blk)),
            full((2, rkv_w)), full((2, LORA_COLS)), full((1, C_WIDTH)), full((1, C_WIDTH)),
            full((1, C_WIDTH)), full((2, C_WIDTH)), full((2, C_WIDTH)),
            full((2, W_LORA, C_WIDTH)), full((2, A_LORA, C_WIDTH)), full((G_LORA, C_WIDTH)),
            full((LANES, LANES)),
        ],
        out_specs=[out_spec] * 11,
        out_shape=[out_shape] * 11,
        compiler_params=_params(("parallel",)),
        name="rwkv_pre",
    )(proj, proj, proj, proj, proj, proj, lp["mu_rkv"], lp["mu_lora"], lp["k_k"], lp["k_a"], lp["r_k"],
      lp["w0"], lp["a0"], lp["w2"], lp["a2"], lp["g2"], ones_bd)


def _expand_heads(x, lane_lo):
    return jnp.concatenate([jnp.where(lane_lo, x, 0.0), jnp.where(lane_lo, 0.0, x)], axis=0).astype(BF16)


def _scan_masks(reverse):
    c = SCAN_CHUNK
    ri = lax.broadcasted_iota(jnp.int32, (2 * c, 2 * c), 0)
    ci = lax.broadcasted_iota(jnp.int32, (2 * c, 2 * c), 1)
    same = (ri // c) == (ci // c)
    tr, tc = ri % c, ci % c
    strict = jnp.logical_and(same, (tc > tr) if reverse else (tc < tr))
    incl = jnp.logical_and(same, (tc >= tr) if reverse else (tc <= tr))
    levels = []
    s = 1
    while s < c:
        late, early = (ci, ri) if reverse else (ri, ci)
        blk = jnp.logical_and((ri // (2 * s)) == (ci // (2 * s)),
                              jnp.logical_and(late % (2 * s) >= s, early % (2 * s) < s))
        levels.append(blk)
        s *= 2
    eye = ri == ci
    return strict, incl, levels, eye


def _scan_chunks(chains):
    c = SCAN_CHUNK
    n = 2 * c
    lane_lo = lax.broadcasted_iota(jnp.int32, (c, LANES), 1) < HEAD_DIM
    n_levels = len(chains[0][8][2])

    prep = []
    for r, v, kk, lw, b, k, cum, state, masks, reverse in chains:
        cum_end = cum[0:1, :] if reverse else cum[c - 1:c, :]
        inv = jnp.exp(-cum)
        tail = jnp.exp(cum_end - cum)
        ar = jnp.concatenate([_expand_heads(kk * jnp.exp(cum - lw), lane_lo),
                              _expand_heads(r * jnp.exp(cum), lane_lo)], axis=0)
        bk = jnp.concatenate([_expand_heads(b * inv, lane_lo), _expand_heads(k * inv, lane_lo)], axis=0)
        bk_tail = jnp.concatenate([_expand_heads(b * tail, lane_lo), _expand_heads(k * tail, lane_lo)], axis=0)
        prep.append((ar, bk, bk_tail, _expand_heads(v, lane_lo), jnp.exp(cum_end)))

    gs = [_dot_nt(ar, bk) for ar, bk, _, _, _ in prep]
    xs = [_dot_nt(p[0], ch[7].astype(BF16)) for p, ch in zip(prep, chains)]
    rhs = [x[:n] + _dot(jnp.where(ch[8][0], g[:n, n:], 0.0).astype(BF16), p[3])
           for x, g, p, ch in zip(xs, gs, prep, chains)]
    t_inv = [jnp.where(ch[8][2][0], -g[:n, :n], jnp.where(ch[8][3], 1.0, 0.0)) for g, ch in zip(gs, chains)]
    for lvl in range(1, n_levels):
        t_b = [t.astype(BF16) for t in t_inv]
        half = [_dot(tb, jnp.where(ch[8][2][lvl], g[:n, :n], 0.0).astype(BF16)).astype(BF16)
                for tb, g, ch in zip(t_b, gs, chains)]
        t_inv = [t - _dot(h, tb) for t, h, tb in zip(t_inv, half, t_b)]
    us = [-_dot(t.astype(BF16), z.astype(BF16)) for t, z in zip(t_inv, rhs)]
    uvs = [jnp.concatenate([u.astype(BF16), p[3]], axis=0) for u, p in zip(us, prep)]
    out = []
    for x, g, uv, p, ch in zip(xs, gs, uvs, prep, chains):
        incl = ch[8][1]
        d_r = jnp.concatenate([jnp.where(incl, g[n:, :n], 0.0), jnp.where(incl, g[n:, n:], 0.0)], axis=1)
        ye = x[n:] + _dot(d_r.astype(BF16), uv)
        out.append(ye[:c] + ye[c:])
    new_states = [ch[7] * p[4] + _dot_tn(uv, p[2]) for uv, p, ch in zip(uvs, prep, chains)]
    return list(zip(out, new_states))


def _scan_kernel(*refs, zero_init):
    fwd, bwd, (tri_f, tri_b) = refs[0:6], refs[6:12], refs[12:14]
    if zero_init:
        yf_ref, yb_ref, sfin_ref, s_ref = refs[14:18]
    else:
        s0_ref, yf_ref, yb_ref, s_ref = refs[14:18]
    head_block = lambda h: (slice(h * HEAD_DIM, (h + 1) * HEAD_DIM),) * 2

    @pl.when(pl.program_id(1) == 0)
    def _():
        s_ref[...] = jnp.zeros(s_ref.shape, F32)
        if not zero_init:
            for d in range(2):
                for p in range(HEAD_PAIRS):
                    for h in range(2):
                        s_ref[(d, p) + head_block(h)] = s0_ref[0, d, 2 * p + h]

    chains, dests = [], []
    for d, (ins, tri, y_ref) in enumerate(((fwd, tri_f, yf_ref), (bwd, tri_b, yb_ref))):
        r_ref, v_ref, kk_ref, lw_ref, b_ref, k_ref = ins
        cum_all = _dot(tri[...], lw_ref[...], precision=HIGHEST)
        masks = _scan_masks(reverse=(d == 1))
        for p in range(HEAD_PAIRS):
            cs = slice(p * LANES, (p + 1) * LANES)
            chains.append((r_ref[:, cs], v_ref[:, cs], kk_ref[:, cs], lw_ref[:, cs], b_ref[:, cs], k_ref[:, cs],
                           cum_all[:, cs], s_ref[d, p], masks, d == 1))
            dests.append((y_ref, d, p, cs))
    for (y_ref, d, p, cs), (y, new_state) in zip(dests, _scan_chunks(chains)):
        y_ref[:, cs] = y
        s_ref[d, p] = new_state

    if zero_init:
        @pl.when(pl.program_id(1) == pl.num_programs(1) - 1)
        def _():
            for d in range(2):
                for p in range(HEAD_PAIRS):
                    for h in range(2):
                        sfin_ref[0, d, 2 * p + h] = s_ref[(d, p) + head_block(h)]


def _rwkv_scan(pre, states, layer, row0, n_seq, t):
    r, v, kk, _, _, lw0, b0, k0, lw1, b1, k1 = pre
    c = SCAN_CHUNK
    nc = t // c
    blk0 = row0 // c
    fspec = pl.BlockSpec((c, C_WIDTH), lambda s, j: (blk0 + s * nc + j, 0))
    bspec = pl.BlockSpec((c, C_WIDTH), lambda s, j: (blk0 + s * nc + nc - 1 - j, 0))
    sblock = (1, 2, C_HEADS, HEAD_DIM, HEAD_DIM)
    tspec = pl.BlockSpec((c, c), lambda s, j: (0, 0))
    tri_f = jnp.tril(jnp.ones((c, c), F32))
    args = [r, v, kk, lw0, b0, k0, r, v, kk, lw1, b1, k1, tri_f, tri_f.T]
    in_specs = [fspec] * 6 + [bspec] * 6 + [tspec, tspec]
    out_specs = [pl.BlockSpec((c, C_WIDTH), lambda s, j: (s * nc + j, 0)),
                 pl.BlockSpec((c, C_WIDTH), lambda s, j: (s * nc + nc - 1 - j, 0))]
    out_shape = [jax.ShapeDtypeStruct((n_seq * t, C_WIDTH), F32)] * 2
    if states is None:
        out_specs.append(pl.BlockSpec(sblock, lambda s, j: (s, 0, 0, 0, 0)))
        out_shape.append(jax.ShapeDtypeStruct((n_seq,) + sblock[1:], F32))
    else:
        args.append(states)
        in_specs.append(pl.BlockSpec(sblock, lambda s, j: (s, layer, 0, 0, 0)))
    return pl.pallas_call(
        functools.partial(_scan_kernel, zero_init=states is None),
        grid=(n_seq, nc),
        in_specs=in_specs,
        out_specs=out_specs,
        out_shape=out_shape,
        scratch_shapes=[pltpu.VMEM((2, HEAD_PAIRS, LANES, LANES), F32)],
        compiler_params=_params(("parallel", "arbitrary")),
        name="rwkv_scan",
    )(*args)


def _rwkv_post_kernel(yfp_ref, ybp_ref, yfs_ref, ybs_ref, bonus_ref, gate_ref, g_ref, b_ref, ones_ref, o_ref,
                      y_scr, *, n_prompt_tiles):
    i = pl.program_id(0)

    @pl.when(i < n_prompt_tiles)
    def _():
        y_scr[...] = yfp_ref[...] + ybp_ref[...]

    @pl.when(i >= n_prompt_tiles)
    def _():
        y_scr[...] = yfs_ref[...] + ybs_ref[...]

    ones_bd = ones_ref[...]
    for p in range(HEAD_PAIRS):
        cs = slice(p * LANES, (p + 1) * LANES)
        y = y_scr[:, cs]
        mu = _head_sums(y, ones_bd) * (1.0 / HEAD_DIM)
        yc = y - mu
        var = _head_sums(yc * yc, ones_bd) * (1.0 / HEAD_DIM)
        yn = yc * lax.rsqrt(var + GN_EPS) * g_ref[:, cs] + b_ref[:, cs]
        o_ref[:, cs] = ((yn + bonus_ref[:, cs]) * gate_ref[:, cs]).astype(BF16)


def _rwkv_post(y_prompt, y_sample, bonus, gate, ln_g, ln_b, ones_bd):
    m = bonus.shape[0]
    n_p = y_prompt[0].shape[0] // POST_TM
    spec = pl.BlockSpec((POST_TM, C_WIDTH), lambda i: (i, 0))
    pspec = pl.BlockSpec((POST_TM, C_WIDTH), lambda i: (jnp.minimum(i, n_p - 1), 0))
    sspec = pl.BlockSpec((POST_TM, C_WIDTH), lambda i: (jnp.maximum(i - n_p, 0), 0))
    vec = pl.BlockSpec((1, C_WIDTH), lambda i: (0, 0))
    return pl.pallas_call(
        functools.partial(_rwkv_post_kernel, n_prompt_tiles=n_p),
        grid=(m // POST_TM,),
        in_specs=[pspec, pspec, sspec, sspec, spec, spec, vec, vec, pl.BlockSpec((LANES, LANES), lambda i: (0, 0))],
        out_specs=spec,
        out_shape=jax.ShapeDtypeStruct((m, C_WIDTH), BF16),
        scratch_shapes=[pltpu.VMEM((POST_TM, C_WIDTH), F32)],
        compiler_params=_params(("parallel",)),
        name="rwkv_post",
    )(*y_prompt, *y_sample, bonus, gate, ln_g, ln_b, ones_bd)


def _merge_kernel(x_ref, attn_p_ref, attn_s_ref, sgu_ref, rw_ref, ga_ref, gb_ref, gc_ref, bg_ref, g1_ref,
                  wa_ref, wb_ref, wc_ref, wo_ref, lg_ref, lb_ref, o_ref, attn_scr, *, n_prompt_tiles):
    i = pl.program_id(0)

    @pl.when(i < n_prompt_tiles)
    def _():
        attn_scr[...] = attn_p_ref[...]

    @pl.when(i >= n_prompt_tiles)
    def _():
        attn_scr[...] = attn_s_ref[...]

    merged = None
    for br, (a_ref, w_ref, g_ref) in enumerate(((attn_scr, wa_ref, ga_ref), (sgu_ref, wb_ref, gb_ref),
                                                 (rw_ref, wc_ref, gc_ref))):
        gate = _sigmoid(g_ref[...] + bg_ref[:, br * D_MODEL:(br + 1) * D_MODEL])
        term = gate * _dot(a_ref[...], w_ref[...])
        merged = term if merged is None else merged + term
    out = _dot(merged.astype(BF16), wo_ref[...])
    y = _layer_norm(ALPHA * x_ref[...] + g1_ref[0] * out)
    o_ref[...] = y * lg_ref[...] + lb_ref[...]


def _merge(x, attn_p, attn_s, sgu, rw, proj, mod3, lp, m_prompt, t_sample):
    m = x.shape[0]
    tm = MERGE_TM
    n_p = m_prompt // tm
    req = functools.partial(_request_of_tile, tm=tm, m_prompt=m_prompt, t_sample=t_sample)
    row = lambda w: pl.BlockSpec((tm, w), lambda i: (i, 0))
    const = lambda shape: pl.BlockSpec(shape, lambda i: (0,) * len(shape), pipeline_mode=pl.Buffered(1))
    gate = lambda br: pl.BlockSpec((tm, D_MODEL), lambda i: (i, CT_GATE * COL_TILE // D_MODEL + br))
    return pl.pallas_call(
        functools.partial(_merge_kernel, n_prompt_tiles=n_p),
        grid=(m // tm,),
        in_specs=[
            row(D_MODEL),
            pl.BlockSpec((tm, A_WIDTH), lambda i: (jnp.minimum(i, n_p - 1), 0)),
            pl.BlockSpec((tm, A_WIDTH), lambda i: (jnp.maximum(i - n_p, 0), 0)),
            row(SGU_WIDTH), row(C_WIDTH), gate(0), gate(1), gate(2),
            const((1, N_BRANCH * D_MODEL)),
            pl.BlockSpec((1, 1, D_MODEL), lambda i: (req(i) * 6 + 2, 0, 0)),
            const((A_WIDTH, D_MODEL)), const((SGU_WIDTH, D_MODEL)), const((C_WIDTH, D_MODEL)),
            const((D_MODEL, D_MODEL)), const((1, D_MODEL)), const((1, D_MODEL)),
        ],
        out_specs=row(D_MODEL),
        out_shape=jax.ShapeDtypeStruct((m, D_MODEL), F32),
        scratch_shapes=[pltpu.VMEM((tm, A_WIDTH), BF16)],
        compiler_params=_params(("parallel",)),
        name="merge",
    )(x, attn_p, attn_s, sgu, rw, proj, proj, proj, lp["b_gate"], mod3, lp["w_branch_a"], lp["w_branch_b"],
      lp["w_branch_c"], lp["w_out"], lp["ln1_g"], lp["ln1_b"])


def _ffn_kernel(x_ref, sh_ref, sc_ref, g2_ref, wg_ref, wu_ref, wo_ref, lg_ref, lb_ref, o_ref, h_scr):
    f = pl.program_id(1)

    @pl.when(f == 0)
    def _():
        h_scr[...] = (_layer_norm(x_ref[...]) * (1.0 + sc_ref[0]) + sh_ref[0]).astype(BF16)
        o_ref[...] = jnp.zeros(o_ref.shape, F32)

    h = h_scr[...]
    g = _dot(h, wg_ref[...])
    u = _dot(h, wu_ref[...])
    o_ref[...] += _dot((g * _sigmoid(g) * u).astype(BF16), wo_ref[...])

    @pl.when(f == pl.num_programs(1) - 1)
    def _():
        y = _layer_norm(ALPHA * x_ref[...] + g2_ref[0] * o_ref[...])
        o_ref[...] = y * lg_ref[...] + lb_ref[...]


def _ffn_dense(x, mod3, w_in, w_out, ln_g, ln_b, m_prompt, t_sample):
    m = x.shape[0]
    tm, tf = FFN_TM, FFN_TF
    nf = D_FF // tf
    req = functools.partial(_request_of_tile, tm=tm, m_prompt=m_prompt, t_sample=t_sample)
    modspec = lambda k: pl.BlockSpec((1, 1, D_MODEL), lambda i, f: (req(i) * 6 + k, 0, 0))
    vec = pl.BlockSpec((1, D_MODEL), lambda i, f: (0, 0))
    return pl.pallas_call(
        _ffn_kernel,
        grid=(m // tm, nf),
        in_specs=[
            pl.BlockSpec((tm, D_MODEL), lambda i, f: (i, 0)),
            modspec(3), modspec(4), modspec(5),
            pl.BlockSpec((D_MODEL, tf), lambda i, f: (0, f)),
            pl.BlockSpec((D_MODEL, tf), lambda i, f: (0, nf + f)),
            pl.BlockSpec((tf, D_MODEL), lambda i, f: (f, 0)),
            vec, vec,
        ],
        out_specs=pl.BlockSpec((tm, D_MODEL), lambda i, f: (i, 0), pipeline_mode=pl.Buffered(1)),
        out_shape=jax.ShapeDtypeStruct((m, D_MODEL), F32),
        scratch_shapes=[pltpu.VMEM((tm, D_MODEL), BF16)],
        compiler_params=_params(("parallel", "arbitrary")),
        name="ffn_dense",
    )(x, mod3, mod3, mod3, w_in, w_in, w_out, ln_g, ln_b)


def _route_kernel(x_ref, sh_ref, sc_ref, wr_ref, h_ref, idx_ref, gate_ref):
    h = _layer_norm(x_ref[...]) * (1.0 + sc_ref[0]) + sh_ref[0]
    h_ref[...] = h
    logits = _dot(h, wr_ref[...], precision=HIGHEST)
    lane = lax.broadcasted_iota(jnp.int32, logits.shape, 1)
    logits = jnp.where(lane < N_EXPERTS, logits, -jnp.inf)
    lane_f = lane.astype(F32)
    m1 = jnp.max(logits, axis=-1, keepdims=True)
    i1 = jnp.min(jnp.where(logits == m1, lane_f, float(LANES)), axis=-1, keepdims=True)
    rest = jnp.where(lane_f == i1, -jnp.inf, logits)
    m2 = jnp.max(rest, axis=-1, keepdims=True)
    i2 = jnp.min(jnp.where(rest == m2, lane_f, float(LANES)), axis=-1, keepdims=True)
    e = jnp.exp(m2 - m1)
    g1 = 1.0 / (1.0 + e)
    idx_ref[...] = jnp.where(lane == 0, i1, jnp.where(lane == 1, i2, 0.0)).astype(jnp.int32)
    gate_ref[...] = jnp.where(lane == 0, g1, jnp.where(lane == 1, e * g1, 0.0))


def _route(x, mod3, router_p, m_prompt, t_sample):
    m = x.shape[0]
    tm = ROUTE_TM
    req = functools.partial(_request_of_tile, tm=tm, m_prompt=m_prompt, t_sample=t_sample)
    modspec = lambda k: pl.BlockSpec((1, 1, D_MODEL), lambda i: (req(i) * 6 + k, 0, 0))
    return pl.pallas_call(
        _route_kernel,
        grid=(m // tm,),
        in_specs=[pl.BlockSpec((tm, D_MODEL), lambda i: (i, 0)), modspec(3), modspec(4),
                  pl.BlockSpec((D_MODEL, LANES), lambda i: (0, 0))],
        out_specs=[pl.BlockSpec((tm, D_MODEL), lambda i: (i, 0)), pl.BlockSpec((tm, LANES), lambda i: (i, 0)),
                   pl.BlockSpec((tm, LANES), lambda i: (i, 0))],
        out_shape=[jax.ShapeDtypeStruct((m, D_MODEL), F32), jax.ShapeDtypeStruct((m, LANES), jnp.int32),
                   jax.ShapeDtypeStruct((m, LANES), F32)],
        compiler_params=_params(("parallel",)),
        name="route",
    )(x, mod3, mod3, router_p)


def _row_copy(src_hbm, row, dst, dst_row, sem):
    return pltpu.make_async_copy(src_hbm.at[pl.ds(row, 1)], dst.at[pl.ds(dst_row, 1)], sem)


def _moe_kernel(tile_e_ref, tile_valid_ref, row_tok_ref, h_hbm, wg_ref, wu_ref, wo_ref, o_ref, x_scr, sem):
    i = pl.program_id(0)
    f = pl.program_id(1)
    valid = tile_valid_ref[i]
    n_sub = lax.shift_right_logical(valid + (MOE_SUB - 1), MOE_SUB.bit_length() - 1)
    n_rows = n_sub * MOE_SUB

    @pl.when(f == 0)
    def _():
        def start(r8, carry):
            for j in range(8):
                r = 8 * r8 + j
                _row_copy(h_hbm, row_tok_ref[0, 0, r], o_ref, r, sem).start(priority=j % 2)
            return carry

        lax.fori_loop(0, n_rows // 8, start, 0)
        for sb in range(MOE_TM // MOE_SUB):
            @pl.when(sb < n_sub)
            def _():
                rows = o_ref.at[pl.ds(sb * MOE_SUB, MOE_SUB)]
                pltpu.make_async_copy(rows, rows, sem).wait()
        for sb in range(MOE_TM // MOE_SUB):
            @pl.when(sb < n_sub)
            def _():
                x_scr[sb * MOE_SUB:(sb + 1) * MOE_SUB, :] = o_ref[sb * MOE_SUB:(sb + 1) * MOE_SUB, :].astype(BF16)
        o_ref[...] = jnp.zeros(o_ref.shape, F32)

    for nb in range(1, MOE_TM // MOE_SUB + 1):
        @pl.when(n_sub == nb)
        def _():
            rows = nb * MOE_SUB
            x = x_scr[:rows, :]
            g = _dot(x, wg_ref[0].astype(BF16))
            u = _dot(x, wu_ref[0].astype(BF16))
            o_ref[:rows, :] += _dot((g * _sigmoid(g) * u).astype(BF16), wo_ref[0].astype(BF16))


def _moe_experts(h, row_tok, tile_e, tile_valid, w_in, w_out):
    n_tiles = tile_e.shape[0]
    nf = D_FF_EXPERT // MOE_TF
    fidx = lambda i, f, tv: jnp.where(tv[i] > 0, f, nf - 1)
    grid_spec = pltpu.PrefetchScalarGridSpec(
        num_scalar_prefetch=2,
        grid=(n_tiles, nf),
        in_specs=[
            pl.BlockSpec((1, 1, MOE_TM), lambda i, f, te, tv: (i, 0, 0), memory_space=pltpu.SMEM),
            pl.BlockSpec(memory_space=pl.ANY),
            pl.BlockSpec((1, D_MODEL, MOE_TF), lambda i, f, te, tv: (te[i], 0, fidx(i, f, tv))),
            pl.BlockSpec((1, D_MODEL, MOE_TF), lambda i, f, te, tv: (te[i], 0, nf + fidx(i, f, tv))),
            pl.BlockSpec((1, MOE_TF, D_MODEL), lambda i, f, te, tv: (te[i], fidx(i, f, tv), 0)),
        ],
        out_specs=pl.BlockSpec((MOE_TM, D_MODEL), lambda i, f, te, tv: (i, 0), pipeline_mode=pl.Buffered(1)),
        scratch_shapes=[pltpu.VMEM((MOE_TM, D_MODEL), BF16), pltpu.SemaphoreType.DMA(())],
    )
    return pl.pallas_call(
        _moe_kernel,
        grid_spec=grid_spec,
        out_shape=jax.ShapeDtypeStruct((n_tiles * MOE_TM, D_MODEL), F32),
        compiler_params=_params(("arbitrary", "arbitrary")),
        name="moe_experts",
    )(tile_e, tile_valid, row_tok.reshape(n_tiles, 1, MOE_TM), h, w_in, w_in, w_out)


def _combine_kernel(pos_ref, pos_next_ref, ys_hbm, x_ref, gate_ref, g2_ref, lg_ref, lb_ref, op_ref, os_ref,
                    buf, sems, *, n_prompt_tiles):
    tm = COMB_TM
    i = pl.program_id(0)
    slot = lax.rem(i, 2)

    def gather(p_ref, s):
        def start(t, carry):
            for k in range(TOP_K):
                _row_copy(ys_hbm, p_ref[0, 0, t * TOP_K + k], buf.at[s, k], t, sems.at[s]).start(priority=k)
            return carry
        lax.fori_loop(0, tm, start, 0, unroll=4)

    @pl.when(i == 0)
    def _():
        gather(pos_ref, 0)

    @pl.when(i + 1 < pl.num_programs(0))
    def _():
        gather(pos_next_ref, 1 - slot)

    pltpu.make_async_copy(buf.at[slot], buf.at[slot], sems.at[slot]).wait()
    y = gate_ref[:, 0:1] * buf[slot, 0] + gate_ref[:, 1:2] * buf[slot, 1]
    z = _layer_norm(ALPHA * x_ref[...] + g2_ref[0] * y)
    out = z * lg_ref[...] + lb_ref[...]

    @pl.when(i < n_prompt_tiles)
    def _():
        op_ref[...] = out

    @pl.when(i >= n_prompt_tiles)
    def _():
        os_ref[...] = out


def _moe_combine(x, ys, pos, gates, mod3, ln_g, ln_b, m_prompt, t_sample):
    m = x.shape[0]
    tm = COMB_TM
    n = m // tm
    n_p = m_prompt // tm
    req = functools.partial(_request_of_tile, tm=tm, m_prompt=m_prompt, t_sample=t_sample)
    pos3 = pos.reshape(n, 1, tm * TOP_K)
    return pl.pallas_call(
        functools.partial(_combine_kernel, n_prompt_tiles=n_p),
        grid=(n,),
        in_specs=[
            pl.BlockSpec((1, 1, tm * TOP_K), lambda i: (i, 0, 0), memory_space=pltpu.SMEM),
            pl.BlockSpec((1, 1, tm * TOP_K), lambda i: (jnp.minimum(i + 1, n - 1), 0, 0), memory_space=pltpu.SMEM),
            pl.BlockSpec(memory_space=pl.ANY),
            pl.BlockSpec((tm, D_MODEL), lambda i: (i, 0)),
            pl.BlockSpec((tm, LANES), lambda i: (i, 0)),
            pl.BlockSpec((1, 1, D_MODEL), lambda i: (req(i) * 6 + 5, 0, 0)),
            pl.BlockSpec((1, D_MODEL), lambda i: (0, 0)),
            pl.BlockSpec((1, D_MODEL), lambda i: (0, 0)),
        ],
        out_specs=[pl.BlockSpec((tm, D_MODEL), lambda i: (jnp.minimum(i, n_p - 1), 0)),
                   pl.BlockSpec((tm, D_MODEL), lambda i: (jnp.maximum(i - n_p, 0), 0))],
        out_shape=[jax.ShapeDtypeStruct((m_prompt, D_MODEL), F32),
                   jax.ShapeDtypeStruct((m - m_prompt, D_MODEL), F32)],
        scratch_shapes=[pltpu.VMEM((2, TOP_K, tm, D_MODEL), F32), pltpu.SemaphoreType.DMA((2,))],
        compiler_params=_params(("arbitrary",)),
        name="moe_combine",
    )(pos3, pos3, ys, x, gates, mod3, ln_g, ln_b)


def _moe_plan(idx):
    m = idx.shape[0]
    flat_e = idx.reshape(-1)
    n_assign = m * TOP_K
    onehot = (flat_e[:, None] == jnp.arange(N_EXPERTS, dtype=jnp.int32)[None, :]).astype(jnp.int32)
    csum = jnp.cumsum(onehot, axis=0)
    counts = csum[-1]
    rank = jnp.take_along_axis(csum, flat_e[:, None], axis=1)[:, 0] - 1
    padded = (counts + MOE_TM - 1) // MOE_TM * MOE_TM
    pad_end = jnp.cumsum(padded)
    pad_start = pad_end - padded
    pos = (pad_start[flat_e] + rank).astype(jnp.int32)
    n_tiles = n_assign // MOE_TM + N_EXPERTS
    row_tok = jnp.zeros((n_tiles * MOE_TM,), jnp.int32).at[pos].set(jnp.arange(n_assign, dtype=jnp.int32) // TOP_K)
    tile_start = jnp.arange(n_tiles, dtype=jnp.int32) * MOE_TM
    tile_e = jnp.sum((pad_end[None, :] <= tile_start[:, None]).astype(jnp.int32), axis=1)
    used = tile_e < N_EXPERTS
    last_e = jnp.max(jnp.where(counts > 0, jnp.arange(N_EXPERTS, dtype=jnp.int32), 0))
    tile_e = jnp.where(used, tile_e, last_e)
    seg_end = pad_start[tile_e] + counts[tile_e]
    tile_valid = jnp.where(used, jnp.clip(seg_end - tile_start, 0, MOE_TM), 0).astype(jnp.int32)
    return row_tok, tile_e, tile_valid, pos


def _reorder_w_in(w_in):
    o_q, o_k, o_v = 0, A_WIDTH, A_WIDTH + A_KV_WIDTH
    o_xb = o_v + A_KV_WIDTH
    o_xc = o_xb + 2 * SGU_WIDTH
    o_lora = o_xc + 3 * C_WIDTH
    o_gl = o_xc + C_COLS
    cols = lambda a, n: w_in[:, :, a:a + n]
    pad = jnp.zeros(w_in.shape[:2] + (COL_TILE - 2 * A_KV_WIDTH - LORA_COLS,), w_in.dtype)
    parts = [cols(o_gl, N_BRANCH * D_MODEL), cols(o_xb, 2 * SGU_WIDTH), cols(o_q, A_WIDTH),
             cols(o_xc, 3 * C_WIDTH), cols(o_k, 2 * A_KV_WIDTH), cols(o_lora, LORA_COLS), pad]
    return jnp.concatenate(parts, axis=-1).astype(BF16)


def kernel(x_prompt, x_sample, cache_k, cache_v, state_rwkv, c, c_ctx, w_mod, b_mod, w_in, b_gate, attn_sink, sgu_norm_g, sgu_w, sgu_b, rwkv_mu, rwkv_w0, rwkv_w2, rwkv_a0, rwkv_a2, rwkv_g2, rwkv_k_k, rwkv_k_a, rwkv_r_k, rwkv_ln_g, rwkv_ln_b, w_branch_a, w_branch_b, w_branch_c, w_out, ln1_g, ln1_b, ln2_g, ln2_b, ffn_w_in, ffn_w_out, moe_router, moe_w_in, moe_w_out):
    bp, tp, _ = x_prompt.shape
    bs, ts, _ = x_sample.shape
    past = cache_k.shape[2]
    mp, ms = bp * tp, bs * ts
    assert bs + 1 <= 16 and mp % IN_TM == 0 and ts % IN_TM == 0 and tp % PRE_TM == 0 and IN_TM % tp == 0

    x = jnp.concatenate([x_prompt.reshape(mp, D_MODEL), x_sample.reshape(ms, D_MODEL)], axis=0)
    cond = jnp.zeros((16, D_MODEL), F32).at[0].set(c_ctx).at[1:1 + bs].set(c)
    mod = _modulation(cond, w_mod, b_mod)

    w_in_p = _reorder_w_in(w_in)
    cos, sin = _rope_tables(ts)
    ones_bd = jnp.kron(jnp.eye(LANES // HEAD_DIM, dtype=F32), jnp.ones((HEAD_DIM, HEAD_DIM), F32))
    cache_k4 = cache_k.reshape(bs, DEPTH, past, A_KV_WIDTH)
    cache_v4 = cache_v.reshape(bs, DEPTH, past, A_KV_WIDTH)
    states4 = state_rwkv.astype(F32).reshape(bs, DEPTH * 2, C_HEADS, HEAD_DIM, HEAD_DIM)

    new_k, new_v, new_s = [], [], []
    for l in range(DEPTH):
        mod3 = mod[l].reshape(16 * 6, 1, D_MODEL)
        lp = {
            "mu_rkv": rwkv_mu[l][:, :3 * C_WIDTH], "mu_lora": rwkv_mu[l][:, 3 * C_WIDTH:],
            "k_k": rwkv_k_k[l].reshape(1, C_WIDTH), "k_a": rwkv_k_a[l].reshape(1, C_WIDTH),
            "r_k": rwkv_r_k[l].reshape(1, C_WIDTH), "w0": rwkv_w0[l], "a0": rwkv_a0[l],
            "w2": rwkv_w2[l].astype(BF16), "a2": rwkv_a2[l].astype(BF16), "g2": rwkv_g2[l].astype(BF16),
            "b_gate": b_gate[l].reshape(1, N_BRANCH * D_MODEL),
            "w_branch_a": w_branch_a[l].astype(BF16), "w_branch_b": w_branch_b[l].astype(BF16),
            "w_branch_c": w_branch_c[l].astype(BF16), "w_out": w_out[l].astype(BF16),
            "ln1_g": ln1_g[l].reshape(1, D_MODEL), "ln1_b": ln1_b[l].reshape(1, D_MODEL),
        }
        proj = _in_proj(x, mod3, w_in_p[l], mp, ts)

        attn_p = _attn_context(proj, attn_sink[l], bp, tp)
        attn_s = _attn_latent(proj, attn_sink[l], cache_k4, cache_v4, l, cos, sin, mp, bs, ts)
        kv_cols = proj[:mp, CT_MISC * COL_TILE:CT_MISC * COL_TILE + 2 * A_KV_WIDTH]
        new_k.append(kv_cols[:, :A_KV_WIDTH].reshape(bp, tp, A_KV_HEADS, HEAD_DIM))
        new_v.append(kv_cols[:, A_KV_WIDTH:].reshape(bp, tp, A_KV_HEADS, HEAD_DIM))

        sgu_bias = jnp.broadcast_to(sgu_b[l][:, :, None], (SGU_GROUPS, SGU_CHUNK, LANES))
        sgu = _sgu(proj, sgu_norm_g[l].reshape(1, SGU_WIDTH), sgu_w[l].astype(BF16), sgu_bias)

        pre = _rwkv_pre(proj, lp, ones_bd, mp, ts)
        yf_p, yb_p, s_fin = _rwkv_scan(pre, None, l, 0, bp, tp)
        yf_s, yb_s = _rwkv_scan(pre, states4, l, mp, bs, ts)
        rw = _rwkv_post((yf_p, yb_p), (yf_s, yb_s), pre[4], pre[3],
                        rwkv_ln_g[l].reshape(1, C_WIDTH), rwkv_ln_b[l].reshape(1, C_WIDTH), ones_bd)
        new_s.append(s_fin)

        x1 = _merge(x, attn_p, attn_s, sgu, rw, proj, mod3, lp, mp, ts)

        ln2g, ln2b = ln2_g[l].reshape(1, D_MODEL), ln2_b[l].reshape(1, D_MODEL)
        if l % 2 == 0:
            x = _ffn_dense(x1, mod3, ffn_w_in[l // 2].astype(BF16), ffn_w_out[l // 2].astype(BF16), ln2g, ln2b, mp, ts)
        else:
            router_p = jnp.zeros((D_MODEL, LANES), F32).at[:, :N_EXPERTS].set(moe_router[l // 2])
            h, idx, gates = _route(x1, mod3, router_p, mp, ts)
            row_tok, tile_e, tile_valid, pos = _moe_plan(idx[:, :TOP_K])
            ys = _moe_experts(h, row_tok, tile_e, tile_valid, moe_w_in[l // 2], moe_w_out[l // 2])
            groups = _moe_combine(x1, ys, pos, gates, mod3, ln2g, ln2b, mp, ts)
            if l + 1 < DEPTH:
                x = jnp.concatenate(groups, axis=0)
    if DEPTH % 2 == 1:
        groups = (x[:mp], x[mp:])

    y_prompt = groups[0].reshape(bp, tp, D_MODEL)
    y_sample = groups[1].reshape(bs, ts, D_MODEL)
    return (y_prompt, y_sample, jnp.stack(new_k, axis=1), jnp.stack(new_v, axis=1), jnp.stack(new_s, axis=1))
```

```python
import functools

import jax
import jax.numpy as jnp
from jax import lax
from jax.experimental import pallas as pl
from jax.experimental.pallas import tpu as pltpu

F32 = jnp.float32
BF16 = jnp.bfloat16
HIGHEST = lax.Precision.HIGHEST

D_MODEL = 2048
DEPTH = 2
GRID_W = 64
A_HEADS = 16
A_KV_HEADS = 4
A_GROUPS = A_HEADS // A_KV_HEADS
HEAD_DIM = 64
A_WIDTH = A_HEADS * HEAD_DIM
A_KV_WIDTH = A_KV_HEADS * HEAD_DIM
ATTN_SCALE = HEAD_DIM ** -0.5
WINDOW = 128
ATTN_BLOCK = 128
ROPE_BASE = 10000.0
SGU_CHUNK = 128
SGU_GROUPS = 8
SGU_WIDTH = 1024
C_HEADS = 16
C_WIDTH = C_HEADS * HEAD_DIM
W_LORA = 64
A_LORA = 64
G_LORA = 128
LORA_COLS = W_LORA + A_LORA + G_LORA
C_COLS = 3 * C_WIDTH + LORA_COLS
N_BRANCH = 3
D_FF = 5632
N_EXPERTS = 8
TOP_K = 2
D_FF_EXPERT = 7168
ALPHA = (2 * DEPTH) ** 0.25
LN_EPS = 1e-6
GN_EPS = HEAD_DIM * 1e-5
NEG_INF = -1e30

LANES = 128
HEAD_PAIRS = C_HEADS * HEAD_DIM // LANES
SCAN_CHUNK = 64
VMEM_LIMIT = 56 * 2 ** 20

COL_TILE = 1024
N_COL_TILES = 13
PROJ_COLS = N_COL_TILES * COL_TILE
CT_GATE, CT_SGU_U, CT_SGU_V, CT_Q, CT_RKV, CT_MISC = 0, 6, 7, 8, 9, 12

IN_TM = 1024
MERGE_TM = 256
FFN_TM = 1024
FFN_TF = 512
PRE_TM = 256
POST_TM = 512
SGU_TM = 512
ROUTE_TM = 512
MOE_TM = 2048
MOE_SUB = 512
MOE_TF = 256
COMB_TM = 256


def _params(sem):
    return pltpu.CompilerParams(dimension_semantics=sem, vmem_limit_bytes=VMEM_LIMIT)


def _dot(a, b, precision=None):
    return jnp.dot(a, b, preferred_element_type=F32, precision=precision)


def _dot_nt(a, b):
    return lax.dot_general(a, b, (((1,), (1,)), ((), ())), preferred_element_type=F32)


def _dot_tn(a, b):
    return lax.dot_general(a, b, (((0,), (0,)), ((), ())), preferred_element_type=F32)


def _layer_norm(x):
    mu = jnp.mean(x, axis=-1, keepdims=True)
    xc = x - mu
    var = jnp.mean(xc * xc, axis=-1, keepdims=True)
    return xc * lax.rsqrt(var + LN_EPS)


def _gelu(x):
    return 0.5 * x * (1.0 + jnp.tanh(0.7978845608028654 * (x + 0.044715 * (x * x * x))))


def _sigmoid(x):
    return 1.0 / (1.0 + jnp.exp(-x))


def _request_of_tile(i, tm, m_prompt, t_sample):
    n_prompt_tiles = m_prompt // tm
    return jnp.where(i < n_prompt_tiles, 0, 1 + (i - n_prompt_tiles) // (t_sample // tm))


def _mod_kernel(c_ref, w_ref, b_ref, o_ref):
    c = c_ref[...]
    h = (c * _sigmoid(c)).astype(BF16)
    o_ref[0] = _dot(h, w_ref[0].astype(BF16)) + b_ref[0]


def _modulation(cond, w_mod, b_mod):
    tn = 1024
    n = w_mod.shape[-1]
    return pl.pallas_call(
        _mod_kernel,
        grid=(DEPTH, n // tn),
        in_specs=[
            pl.BlockSpec((16, D_MODEL), lambda l, j: (0, 0)),
            pl.BlockSpec((1, D_MODEL, tn), lambda l, j: (l, 0, j)),
            pl.BlockSpec((1, 1, tn), lambda l, j: (l, 0, j)),
        ],
        out_specs=pl.BlockSpec((1, 16, tn), lambda l, j: (l, 0, j)),
        out_shape=jax.ShapeDtypeStruct((DEPTH, 16, n), F32),
        compiler_params=_params(("parallel", "parallel")),
        name="modulation",
    )(cond, w_mod, b_mod.reshape(DEPTH, 1, n))


def _in_kernel(x_ref, sh_ref, sc_ref, w_ref, o_ref, h_scr):
    @pl.when(pl.program_id(1) == 0)
    def _():
        y = _layer_norm(x_ref[...])
        h_scr[...] = (y * (1.0 + sc_ref[0]) + sh_ref[0]).astype(BF16)

    o_ref[...] = _dot(h_scr[...], w_ref[...])


def _in_proj(x, mod3, w_in_p, m_prompt, t_sample):
    m = x.shape[0]
    req = functools.partial(_request_of_tile, tm=IN_TM, m_prompt=m_prompt, t_sample=t_sample)
    return pl.pallas_call(
        _in_kernel,
        grid=(m // IN_TM, N_COL_TILES),
        in_specs=[
            pl.BlockSpec((IN_TM, D_MODEL), lambda i, j: (i, 0)),
            pl.BlockSpec((1, 1, D_MODEL), lambda i, j: (req(i) * 6 + 0, 0, 0)),
            pl.BlockSpec((1, 1, D_MODEL), lambda i, j: (req(i) * 6 + 1, 0, 0)),
            pl.BlockSpec((D_MODEL, COL_TILE), lambda i, j: (0, j)),
        ],
        out_specs=pl.BlockSpec((IN_TM, COL_TILE), lambda i, j: (i, j)),
        out_shape=jax.ShapeDtypeStruct((m, PROJ_COLS), F32),
        scratch_shapes=[pltpu.VMEM((IN_TM, D_MODEL), BF16)],
        compiler_params=_params(("parallel", "arbitrary")),
        name="in_proj",
    )(x, mod3, mod3, w_in_p)


def _softmax_parts(scores, sink):
    m = sink
    for s in scores:
        m = jnp.maximum(m, jnp.max(s, axis=-1, keepdims=True))
    ps = [jnp.exp(s - m) for s in scores]
    den = jnp.exp(sink - m)
    for p in ps:
        den = den + jnp.sum(p, axis=-1, keepdims=True)
    inv = 1.0 / den
    return [(p * inv).astype(BF16) for p in ps]


def _gqa_attention(q, key_sets, sink_ref, o_ref):
    rows = q.shape[0]
    col = lambda kvh: slice(kvh * HEAD_DIM, (kvh + 1) * HEAD_DIM)
    heads = lambda kvh: range(kvh * A_GROUPS, (kvh + 1) * A_GROUPS)
    masks = [m for _, _, m in key_sets]
    qs = [jnp.concatenate([q[:, h * HEAD_DIM:(h + 1) * HEAD_DIM] for h in heads(kvh)], axis=0).astype(BF16)
          for kvh in range(A_KV_HEADS)]
    scores = []
    for kvh in range(A_KV_HEADS):
        per_set = []
        for (k, _, _), m in zip(key_sets, masks):
            s = _dot_nt(qs[kvh], k[:, col(kvh)]) * ATTN_SCALE
            per_set.append(s if m is None else jnp.where(m, s, NEG_INF))
        scores.append(per_set)
    probs = []
    for kvh in range(A_KV_HEADS):
        sink = jnp.concatenate([jnp.full((rows, 1), sink_ref[h], F32) for h in heads(kvh)], axis=0)
        probs.append(_softmax_parts(scores[kvh], sink))
    for kvh in range(A_KV_HEADS):
        o = None
        for (_, v, _), p in zip(key_sets, probs[kvh]):
            term = _dot(p, v[:, col(kvh)])
            o = term if o is None else o + term
        for g, h in enumerate(heads(kvh)):
            o_ref[:, h * HEAD_DIM:(h + 1) * HEAD_DIM] = o[g * rows:(g + 1) * rows].astype(BF16)


def _attn_ctx_kernel(sink_ref, q_ref, kv_ref, o_ref):
    k = kv_ref[:, :A_KV_WIDTH].astype(BF16)
    v = kv_ref[:, A_KV_WIDTH:].astype(BF16)
    _gqa_attention(q_ref[...], [(k, v, None)], sink_ref, o_ref)


def _attn_context(proj, sink, n_seq, t):
    kv_w = 2 * A_KV_WIDTH
    return pl.pallas_call(
        _attn_ctx_kernel,
        grid=(n_seq,),
        in_specs=[
            pl.BlockSpec(memory_space=pltpu.SMEM),
            pl.BlockSpec((t, A_WIDTH), lambda b: (b, CT_Q)),
            pl.BlockSpec((t, kv_w), lambda b: (b, CT_MISC * COL_TILE // kv_w)),
        ],
        out_specs=pl.BlockSpec((t, A_WIDTH), lambda b: (b, 0)),
        out_shape=jax.ShapeDtypeStruct((n_seq * t, A_WIDTH), BF16),
        compiler_params=_params(("parallel",)),
        name="attn_context",
    )(sink, proj, proj)


def _rope(x, cos, sin_signed):
    w = x.shape[-1]
    lane = lax.broadcasted_iota(jnp.int32, x.shape, 1)
    swapped = jnp.where((lane % 32) < 16, pltpu.roll(x, w - 16, 1), pltpu.roll(x, 16, 1))
    return x * cos + swapped * sin_signed


def _attn_lat_kernel(sink_ref, q_ref, kv_ref, cq_ref, sq_ref, ck_ref, sk_ref, kc_ref, vc_ref, o_ref,
                     k_scr, v_scr, kc_scr, vc_scr, *, t):
    n = pl.program_id(1)
    span = ATTN_BLOCK + 2 * WINDOW

    @pl.when(n == 0)
    def _():
        k_scr[...] = _rope(kv_ref[:, :A_KV_WIDTH], ck_ref[...], sk_ref[...]).astype(BF16)
        v_scr[...] = kv_ref[:, A_KV_WIDTH:].astype(BF16)
        kc_scr[...] = kc_ref[0, 0].astype(BF16)
        vc_scr[...] = vc_ref[0, 0].astype(BF16)

    q = _rope(q_ref[...], cq_ref[...], sq_ref[...])
    start = pl.multiple_of(jnp.clip(n * ATTN_BLOCK - WINDOW, 0, t - span), ATTN_BLOCK)
    k_loc = k_scr[pl.ds(start, span), :]
    v_loc = v_scr[pl.ds(start, span), :]
    stacked = (A_GROUPS * ATTN_BLOCK, span)
    q_pos = n * ATTN_BLOCK + lax.broadcasted_iota(jnp.int32, stacked, 0) % ATTN_BLOCK
    k_pos = start + lax.broadcasted_iota(jnp.int32, stacked, 1)
    valid = jnp.logical_and(q_pos - k_pos <= WINDOW, k_pos - q_pos <= WINDOW)
    _gqa_attention(q, [(k_loc, v_loc, valid), (kc_scr[...], vc_scr[...], None)], sink_ref, o_ref)


def _attn_latent(proj, sink, cache_k, cache_v, layer, cos, sin, m_prompt, n_seq, t):
    kv_w = 2 * A_KV_WIDTH
    nb = t // ATTN_BLOCK
    past = cache_k.shape[2]
    row0 = m_prompt // ATTN_BLOCK
    seq0 = m_prompt // t
    return pl.pallas_call(
        functools.partial(_attn_lat_kernel, t=t),
        grid=(n_seq, nb),
        in_specs=[
            pl.BlockSpec(memory_space=pltpu.SMEM),
            pl.BlockSpec((ATTN_BLOCK, A_WIDTH), lambda b, n: (row0 + b * nb + n, CT_Q)),
            pl.BlockSpec((t, kv_w), lambda b, n: (seq0 + b, CT_MISC * COL_TILE // kv_w)),
            pl.BlockSpec((ATTN_BLOCK, A_WIDTH), lambda b, n: (n, 0)),
            pl.BlockSpec((ATTN_BLOCK, A_WIDTH), lambda b, n: (n, 0)),
            pl.BlockSpec((t, A_KV_WIDTH), lambda b, n: (0, 0)),
            pl.BlockSpec((t, A_KV_WIDTH), lambda b, n: (0, 0)),
            pl.BlockSpec((1, 1, past, A_KV_WIDTH), lambda b, n: (b, layer, 0, 0)),
            pl.BlockSpec((1, 1, past, A_KV_WIDTH), lambda b, n: (b, layer, 0, 0)),
        ],
        out_specs=pl.BlockSpec((ATTN_BLOCK, A_WIDTH), lambda b, n: (b * nb + n, 0)),
        out_shape=jax.ShapeDtypeStruct((n_seq * t, A_WIDTH), BF16),
        scratch_shapes=[
            pltpu.VMEM((t, A_KV_WIDTH), BF16),
            pltpu.VMEM((t, A_KV_WIDTH), BF16),
            pltpu.VMEM((past, A_KV_WIDTH), BF16),
            pltpu.VMEM((past, A_KV_WIDTH), BF16),
        ],
        compiler_params=_params(("parallel", "arbitrary")),
        name="attn_latent",
    )(sink, proj, proj, cos, sin, cos, sin, cache_k, cache_v)


def _rope_tables(t):
    pos = jnp.arange(t)
    half = HEAD_DIM // 2
    freqs = ROPE_BASE ** (-jnp.arange(0, half, 2, dtype=F32) / half)

    def tab(p):
        ang = p.astype(F32)[:, None] * freqs[None, :]
        c, s = jnp.cos(ang), jnp.sin(ang)
        return jnp.concatenate([c, c], -1), jnp.concatenate([-s, s], -1)

    c_row, s_row = tab(pos // GRID_W)
    c_col, s_col = tab(pos % GRID_W)
    cos = jnp.concatenate([c_row, c_col], -1)
    sin = jnp.concatenate([s_row, s_col], -1)
    return jnp.tile(cos, (1, A_HEADS)), jnp.tile(sin, (1, A_HEADS))


def _sgu_kernel(u_ref, v_ref, ng_ref, w_ref, b_ref, o_ref):
    for ch in range(SGU_TM // SGU_CHUNK):
        rs = slice(ch * SGU_CHUNK, (ch + 1) * SGU_CHUNK)
        for g in range(SGU_GROUPS):
            cs = slice(g * LANES, (g + 1) * LANES)
            vn = _layer_norm(_gelu(v_ref[rs, cs])) * ng_ref[:, cs]
            mixed = _dot(w_ref[g], vn.astype(BF16)) + b_ref[g]
            o_ref[rs, cs] = (_gelu(u_ref[rs, cs]) * mixed).astype(BF16)


def _sgu(proj, norm_g, w_s, bias):
    m = proj.shape[0]
    return pl.pallas_call(
        _sgu_kernel,
        grid=(m // SGU_TM,),
        in_specs=[
            pl.BlockSpec((SGU_TM, SGU_WIDTH), lambda i: (i, CT_SGU_U)),
            pl.BlockSpec((SGU_TM, SGU_WIDTH), lambda i: (i, CT_SGU_V)),
            pl.BlockSpec((1, SGU_WIDTH), lambda i: (0, 0)),
            pl.BlockSpec((SGU_GROUPS, SGU_CHUNK, SGU_CHUNK), lambda i: (0, 0, 0)),
            pl.BlockSpec((SGU_GROUPS, SGU_CHUNK, LANES), lambda i: (0, 0, 0)),
        ],
        out_specs=pl.BlockSpec((SGU_TM, SGU_WIDTH), lambda i: (i, 0)),
        out_shape=jax.ShapeDtypeStruct((m, SGU_WIDTH), BF16),
        compiler_params=_params(("parallel",)),
        name="sgu",
    )(proj, proj, norm_g, w_s, bias)


def _head_sums(x, ones_bd):
    hi = x.astype(BF16)
    lo = (x - hi.astype(F32)).astype(BF16)
    return _dot(hi, ones_bd) + _dot(lo, ones_bd)


def _rwkv_pre_kernel(x_ref, xp_ref, xn_ref, lo_ref, lop_ref, lon_ref, mu_ref, mul_ref, kk_ref, ka_ref,
                     rk_ref, w0_ref, a0_ref, w2_ref, a2_ref, g2_ref, ones_ref,
                     r_ref, v_ref, kkn_ref, gate_ref, bonus_ref,
                     lw0_ref, b0_ref, k0_ref, lw1_ref, b1_ref, k1_ref, *, n_prompt_tiles, tiles_per_seq):
    i = pl.program_id(0)
    is_prompt = i < n_prompt_tiles
    j = lax.rem(jnp.maximum(i - n_prompt_tiles, 0), tiles_per_seq)
    first = jnp.logical_or(is_prompt, j == 0)
    last = jnp.logical_or(is_prompt, j == tiles_per_seq - 1)

    def shift(x, xp, xn, mu):
        tm = x.shape[0]
        row = lax.broadcasted_iota(jnp.int32, x.shape, 0)
        prev_row = jnp.where(first, 0.0, xp[7:8, :])
        next_row = jnp.where(last, 0.0, xn[0:1, :])
        prev = jnp.where(row == 0, prev_row, pltpu.roll(x, 1, 0))
        nxt = jnp.where(row == tm - 1, next_row, pltpu.roll(x, tm - 1, 0))
        return x + mu[0:1, :] * (prev - x) + mu[1:2, :] * (nxt - x)

    xs = shift(x_ref[...], xp_ref[...], xn_ref[...], mu_ref[...])
    lo = shift(lo_ref[...], lop_ref[...], lon_ref[...], mul_ref[...])
    r = xs[:, :C_WIDTH]
    k = xs[:, C_WIDTH:2 * C_WIDTH]
    v = xs[:, 2 * C_WIDTH:]
    tw = jnp.tanh(lo[:, :W_LORA]).astype(BF16)
    al = lo[:, W_LORA:W_LORA + A_LORA].astype(BF16)
    gl = _sigmoid(lo[:, W_LORA + A_LORA:]).astype(BF16)
    ones_bd = ones_ref[...]

    r_ref[...] = r
    v_ref[...] = v
    gate_ref[...] = _dot(gl, g2_ref[...])
    kk0 = k * kk_ref[...]
    for p in range(HEAD_PAIRS):
        cs = slice(p * LANES, (p + 1) * LANES)
        ss = _head_sums(kk0[:, cs] * kk0[:, cs], ones_bd)
        kkn_ref[:, cs] = kk0[:, cs] * lax.rsqrt(jnp.maximum(ss, 1e-24))
    kkn = kkn_ref[...]

    bonus = None
    for d, (lw_ref, b_ref, kd_ref) in enumerate(((lw0_ref, b0_ref, k0_ref), (lw1_ref, b1_ref, k1_ref))):
        z = w0_ref[d:d + 1, :] + _dot(tw, w2_ref[d])
        softplus_neg = jnp.maximum(-z, 0.0) + jnp.log(1.0 + jnp.exp(-jnp.abs(z)))
        lw_ref[...] = -jnp.exp(-softplus_neg - 0.5)
        a = _sigmoid(a0_ref[d:d + 1, :] + _dot(al, a2_ref[d]))
        kd = k * (1.0 + (a - 1.0) * ka_ref[...])
        kd_ref[...] = kd
        b_ref[...] = kkn * a
        rkr = r * kd * rk_ref[...]
        parts = [_head_sums(rkr[:, p * LANES:(p + 1) * LANES], ones_bd) for p in range(HEAD_PAIRS)]
        term = jnp.concatenate(parts, axis=1) * v
        bonus = term if bonus is None else bonus + term
    bonus_ref[...] = bonus


def _rwkv_pre(proj, lp, ones_bd, m_prompt, t_sample):
    m = proj.shape[0]
    tm = PRE_TM
    rkv_w = 3 * C_WIDTH
    lo_blk = (CT_MISC * COL_TILE + 2 * A_KV_WIDTH) // LORA_COLS
    n8 = m // 8
    prev8 = lambda i: jnp.maximum(i * (tm // 8) - 1, 0)
    next8 = lambda i: jnp.minimum((i + 1) * (tm // 8), n8 - 1)
    full = lambda shape: pl.BlockSpec(shape, lambda i: (0,) * len(shape))
    out_spec = pl.BlockSpec((tm, C_WIDTH), lambda i: (i, 0))
    out_shape = jax.ShapeDtypeStruct((m, C_WIDTH), F32)
    kern = functools.partial(_rwkv_pre_kernel, n_prompt_tiles=m_prompt // tm, tiles_per_seq=t_sample // tm)
    return pl.pallas_call(
        kern,
        grid=(m // tm,),
        in_specs=[
            pl.BlockSpec((tm, rkv_w), lambda i: (i, CT_RKV * COL_TILE // rkv_w)),
            pl.BlockSpec((8, rkv_w), lambda i: (prev8(i), CT_RKV * COL_TILE // rkv_w)),
            pl.BlockSpec((8, rkv_w), lambda i: (next8(i), CT_RKV * COL_TILE // rkv_w)),
            pl.BlockSpec((tm, LORA_COLS), lambda i: (i, lo_blk)),
            pl.BlockSpec((8, LORA_COLS), lambda i: (prev8(i), lo_blk)),
            pl.BlockSpec((8, LORA_COLS), lambda i: (next8(i), lo_blk)),
            full((2, rkv_w)), full((2, LORA_COLS)), full((1, C_WIDTH)), full((1, C_WIDTH)),
            full((1, C_WIDTH)), full((2, C_WIDTH)), full((2, C_WIDTH)),
            full((2, W_LORA, C_WIDTH)), full((2, A_LORA, C_WIDTH)), full((G_LORA, C_WIDTH)),
            full((LANES, LANES)),
        ],
        out_specs=[out_spec] * 11,
        out_shape=[out_shape] * 11,
        compiler_params=_params(("parallel",)),
        name="rwkv_pre",
    )(proj, proj, proj, proj, proj, proj, lp["mu_rkv"], lp["mu_lora"], lp["k_k"], lp["k_a"], lp["r_k"],
      lp["w0"], lp["a0"], lp["w2"], lp["a2"], lp["g2"], ones_bd)


def _expand_heads(x, lane_lo):
    return jnp.concatenate([jnp.where(lane_lo, x, 0.0), jnp.where(lane_lo, 0.0, x)], axis=0).astype(BF16)


def _scan_masks(reverse):
    c = SCAN_CHUNK
    ri = lax.broadcasted_iota(jnp.int32, (2 * c, 2 * c), 0)
    ci = lax.broadcasted_iota(jnp.int32, (2 * c, 2 * c), 1)
    same = (ri // c) == (ci // c)
    tr, tc = ri % c, ci % c
    strict = jnp.logical_and(same, (tc > tr) if reverse else (tc < tr))
    incl = jnp.logical_and(same, (tc >= tr) if reverse else (tc <= tr))
    levels = []
    s = 1
    while s < c:
        late, early = (ci, ri) if reverse else (ri, ci)
        blk = jnp.logical_and((ri // (2 * s)) == (ci // (2 * s)),
                              jnp.logical_and(late % (2 * s) >= s, early % (2 * s) < s))
        levels.append(blk)
        s *= 2
    eye = ri == ci
    return strict, incl, levels, eye


def _scan_chunks(chains):
    c = SCAN_CHUNK
    n = 2 * c
    lane_lo = lax.broadcasted_iota(jnp.int32, (c, LANES), 1) < HEAD_DIM
    n_levels = len(chains[0][8][2])

    prep = []
    for r, v, kk, lw, b, k, cum, state, masks, reverse in chains:
        cum_end = cum[0:1, :] if reverse else cum[c - 1:c, :]
        inv = jnp.exp(-cum)
        tail = jnp.exp(cum_end - cum)
        ar = jnp.concatenate([_expand_heads(kk * jnp.exp(cum - lw), lane_lo),
                              _expand_heads(r * jnp.exp(cum), lane_lo)], axis=0)
        bk = jnp.concatenate([_expand_heads(b * inv, lane_lo), _expand_heads(k * inv, lane_lo)], axis=0)
        bk_tail = jnp.concatenate([_expand_heads(b * tail, lane_lo), _expand_heads(k * tail, lane_lo)], axis=0)
        prep.append((ar, bk, bk_tail, _expand_heads(v, lane_lo), jnp.exp(cum_end)))

    gs = [_dot_nt(ar, bk) for ar, bk, _, _, _ in prep]
    xs = [_dot_nt(p[0], ch[7].astype(BF16)) for p, ch in zip(prep, chains)]
    rhs = [x[:n] + _dot(jnp.where(ch[8][0], g[:n, n:], 0.0).astype(BF16), p[3])
           for x, g, p, ch in zip(xs, gs, prep, chains)]
    t_inv = [jnp.where(ch[8][2][0], -g[:n, :n], jnp.where(ch[8][3], 1.0, 0.0)) for g, ch in zip(gs, chains)]
    for lvl in range(1, n_levels):
        t_b = [t.astype(BF16) for t in t_inv]
        half = [_dot(tb, jnp.where(ch[8][2][lvl], g[:n, :n], 0.0).astype(BF16)).astype(BF16)
                for tb, g, ch in zip(t_b, gs, chains)]
        t_inv = [t - _dot(h, tb) for t, h, tb in zip(t_inv, half, t_b)]
    us = [-_dot(t.astype(BF16), z.astype(BF16)) for t, z in zip(t_inv, rhs)]
    uvs = [jnp.concatenate([u.astype(BF16), p[3]], axis=0) for u, p in zip(us, prep)]
    out = []
    for x, g, uv, p, ch in zip(xs, gs, uvs, prep, chains):
        incl = ch[8][1]
        d_r = jnp.concatenate([jnp.where(incl, g[n:, :n], 0.0), jnp.where(incl, g[n:, n:], 0.0)], axis=1)
        ye = x[n:] + _dot(d_r.astype(BF16), uv)
        out.append(ye[:c] + ye[c:])
    new_states = [ch[7] * p[4] + _dot_tn(uv, p[2]) for uv, p, ch in zip(uvs, prep, chains)]
    return list(zip(out, new_states))


def _scan_kernel(*refs, zero_init):
    fwd, bwd, (tri_f, tri_b) = refs[0:6], refs[6:12], refs[12:14]
    if zero_init:
        yf_ref, yb_ref, sfin_ref, s_ref = refs[14:18]
    else:
        s0_ref, yf_ref, yb_ref, s_ref = refs[14:18]
    head_block = lambda h: (slice(h * HEAD_DIM, (h + 1) * HEAD_DIM),) * 2

    @pl.when(pl.program_id(1) == 0)
    def _():
        s_ref[...] = jnp.zeros(s_ref.shape, F32)
        if not zero_init:
            for d in range(2):
                for p in range(HEAD_PAIRS):
                    for h in range(2):
                        s_ref[(d, p) + head_block(h)] = s0_ref[0, d, 2 * p + h]

    chains, dests = [], []
    for d, (ins, tri, y_ref) in enumerate(((fwd, tri_f, yf_ref), (bwd, tri_b, yb_ref))):
        r_ref, v_ref, kk_ref, lw_ref, b_ref, k_ref = ins
        cum_all = _dot(tri[...], lw_ref[...], precision=HIGHEST)
        masks = _scan_masks(reverse=(d == 1))
        for p in range(HEAD_PAIRS):
            cs = slice(p * LANES, (p + 1) * LANES)
            chains.append((r_ref[:, cs], v_ref[:, cs], kk_ref[:, cs], lw_ref[:, cs], b_ref[:, cs], k_ref[:, cs],
                           cum_all[:, cs], s_ref[d, p], masks, d == 1))
            dests.append((y_ref, d, p, cs))
    for (y_ref, d, p, cs), (y, new_state) in zip(dests, _scan_chunks(chains)):
        y_ref[:, cs] = y
        s_ref[d, p] = new_state

    if zero_init:
        @pl.when(pl.program_id(1) == pl.num_programs(1) - 1)
        def _():
            for d in range(2):
                for p in range(HEAD_PAIRS):
                    for h in range(2):
                        sfin_ref[0, d, 2 * p + h] = s_ref[(d, p) + head_block(h)]


def _rwkv_scan(pre, states, layer, row0, n_seq, t):
    r, v, kk, _, _, lw0, b0, k0, lw1, b1, k1 = pre
    c = SCAN_CHUNK
    nc = t // c
    blk0 = row0 // c
    fspec = pl.BlockSpec((c, C_WIDTH), lambda s, j: (blk0 + s * nc + j, 0))
    bspec = pl.BlockSpec((c, C_WIDTH), lambda s, j: (blk0 + s * nc + nc - 1 - j, 0))
    sblock = (1, 2, C_HEADS, HEAD_DIM, HEAD_DIM)
    tspec = pl.BlockSpec((c, c), lambda s, j: (0, 0))
    tri_f = jnp.tril(jnp.ones((c, c), F32))
    args = [r, v, kk, lw0, b0, k0, r, v, kk, lw1, b1, k1, tri_f, tri_f.T]
    in_specs = [fspec] * 6 + [bspec] * 6 + [tspec, tspec]
    out_specs = [pl.BlockSpec((c, C_WIDTH), lambda s, j: (s * nc + j, 0)),
                 pl.BlockSpec((c, C_WIDTH), lambda s, j: (s * nc + nc - 1 - j, 0))]
    out_shape = [jax.ShapeDtypeStruct((n_seq * t, C_WIDTH), F32)] * 2
    if states is None:
        out_specs.append(pl.BlockSpec(sblock, lambda s, j: (s, 0, 0, 0, 0)))
        out_shape.append(jax.ShapeDtypeStruct((n_seq,) + sblock[1:], F32))
    else:
        args.append(states)
        in_specs.append(pl.BlockSpec(sblock, lambda s, j: (s, layer, 0, 0, 0)))
    return pl.pallas_call(
        functools.partial(_scan_kernel, zero_init=states is None),
        grid=(n_seq, nc),
        in_specs=in_specs,
        out_specs=out_specs,
        out_shape=out_shape,
        scratch_shapes=[pltpu.VMEM((2, HEAD_PAIRS, LANES, LANES), F32)],
        compiler_params=_params(("parallel", "arbitrary")),
        name="rwkv_scan",
    )(*args)


def _rwkv_post_kernel(yfp_ref, ybp_ref, yfs_ref, ybs_ref, bonus_ref, gate_ref, g_ref, b_ref, ones_ref, o_ref,
                      y_scr, *, n_prompt_tiles):
    i = pl.program_id(0)

    @pl.when(i < n_prompt_tiles)
    def _():
        y_scr[...] = yfp_ref[...] + ybp_ref[...]

    @pl.when(i >= n_prompt_tiles)
    def _():
        y_scr[...] = yfs_ref[...] + ybs_ref[...]

    ones_bd = ones_ref[...]
    for p in range(HEAD_PAIRS):
        cs = slice(p * LANES, (p + 1) * LANES)
        y = y_scr[:, cs]
        mu = _head_sums(y, ones_bd) * (1.0 / HEAD_DIM)
        yc = y - mu
        var = _head_sums(yc * yc, ones_bd) * (1.0 / HEAD_DIM)
        yn = yc * lax.rsqrt(var + GN_EPS) * g_ref[:, cs] + b_ref[:, cs]
        o_ref[:, cs] = ((yn + bonus_ref[:, cs]) * gate_ref[:, cs]).astype(BF16)


def _rwkv_post(y_prompt, y_sample, bonus, gate, ln_g, ln_b, ones_bd):
    m = bonus.shape[0]
    n_p = y_prompt[0].shape[0] // POST_TM
    spec = pl.BlockSpec((POST_TM, C_WIDTH), lambda i: (i, 0))
    pspec = pl.BlockSpec((POST_TM, C_WIDTH), lambda i: (jnp.minimum(i, n_p - 1), 0))
    sspec = pl.BlockSpec((POST_TM, C_WIDTH), lambda i: (jnp.maximum(i - n_p, 0), 0))
    vec = pl.BlockSpec((1, C_WIDTH), lambda i: (0, 0))
    return pl.pallas_call(
        functools.partial(_rwkv_post_kernel, n_prompt_tiles=n_p),
        grid=(m // POST_TM,),
        in_specs=[pspec, pspec, sspec, sspec, spec, spec, vec, vec, pl.BlockSpec((LANES, LANES), lambda i: (0, 0))],
        out_specs=spec,
        out_shape=jax.ShapeDtypeStruct((m, C_WIDTH), BF16),
        scratch_shapes=[pltpu.VMEM((POST_TM, C_WIDTH), F32)],
        compiler_params=_params(("parallel",)),
        name="rwkv_post",
    )(*y_prompt, *y_sample, bonus, gate, ln_g, ln_b, ones_bd)


def _merge_kernel(x_ref, attn_p_ref, attn_s_ref, sgu_ref, rw_ref, ga_ref, gb_ref, gc_ref, bg_ref, g1_ref,
                  wa_ref, wb_ref, wc_ref, wo_ref, lg_ref, lb_ref, o_ref, attn_scr, *, n_prompt_tiles):
    i = pl.program_id(0)

    @pl.when(i < n_prompt_tiles)
    def _():
        attn_scr[...] = attn_p_ref[...]

    @pl.when(i >= n_prompt_tiles)
    def _():
        attn_scr[...] = attn_s_ref[...]

    merged = None
    for br, (a_ref, w_ref, g_ref) in enumerate(((attn_scr, wa_ref, ga_ref), (sgu_ref, wb_ref, gb_ref),
                                                 (rw_ref, wc_ref, gc_ref))):
        gate = _sigmoid(g_ref[...] + bg_ref[:, br * D_MODEL:(br + 1) * D_MODEL])
        term = gate * _dot(a_ref[...], w_ref[...])
        merged = term if merged is None else merged + term
    out = _dot(merged.astype(BF16), wo_ref[...])
    y = _layer_norm(ALPHA * x_ref[...] + g1_ref[0] * out)
    o_ref[...] = y * lg_ref[...] + lb_ref[...]


def _merge(x, attn_p, attn_s, sgu, rw, proj, mod3, lp, m_prompt, t_sample):
    m = x.shape[0]
    tm = MERGE_TM
    n_p = m_prompt // tm
    req = functools.partial(_request_of_tile, tm=tm, m_prompt=m_prompt, t_sample=t_sample)
    row = lambda w: pl.BlockSpec((tm, w), lambda i: (i, 0))
    const = lambda shape: pl.BlockSpec(shape, lambda i: (0,) * len(shape), pipeline_mode=pl.Buffered(1))
    gate = lambda br: pl.BlockSpec((tm, D_MODEL), lambda i: (i, CT_GATE * COL_TILE // D_MODEL + br))
    return pl.pallas_call(
        functools.partial(_merge_kernel, n_prompt_tiles=n_p),
        grid=(m // tm,),
        in_specs=[
            row(D_MODEL),
            pl.BlockSpec((tm, A_WIDTH), lambda i: (jnp.minimum(i, n_p - 1), 0)),
            pl.BlockSpec((tm, A_WIDTH), lambda i: (jnp.maximum(i - n_p, 0), 0)),
            row(SGU_WIDTH), row(C_WIDTH), gate(0), gate(1), gate(2),
            const((1, N_BRANCH * D_MODEL)),
            pl.BlockSpec((1, 1, D_MODEL), lambda i: (req(i) * 6 + 2, 0, 0)),
            const((A_WIDTH, D_MODEL)), const((SGU_WIDTH, D_MODEL)), const((C_WIDTH, D_MODEL)),
            const((D_MODEL, D_MODEL)), const((1, D_MODEL)), const((1, D_MODEL)),
        ],
        out_specs=row(D_MODEL),
        out_shape=jax.ShapeDtypeStruct((m, D_MODEL), F32),
        scratch_shapes=[pltpu.VMEM((tm, A_WIDTH), BF16)],
        compiler_params=_params(("parallel",)),
        name="merge",
    )(x, attn_p, attn_s, sgu, rw, proj, proj, proj, lp["b_gate"], mod3, lp["w_branch_a"], lp["w_branch_b"],
      lp["w_branch_c"], lp["w_out"], lp["ln1_g"], lp["ln1_b"])


def _ffn_kernel(x_ref, sh_ref, sc_ref, g2_ref, wg_ref, wu_ref, wo_ref, lg_ref, lb_ref, o_ref, h_scr):
    f = pl.program_id(1)

    @pl.when(f == 0)
    def _():
        h_scr[...] = (_layer_norm(x_ref[...]) * (1.0 + sc_ref[0]) + sh_ref[0]).astype(BF16)
        o_ref[...] = jnp.zeros(o_ref.shape, F32)

    h = h_scr[...]
    g = _dot(h, wg_ref[...])
    u = _dot(h, wu_ref[...])
    o_ref[...] += _dot((g * _sigmoid(g) * u).astype(BF16), wo_ref[...])

    @pl.when(f == pl.num_programs(1) - 1)
    def _():
        y = _layer_norm(ALPHA * x_ref[...] + g2_ref[0] * o_ref[...])
        o_ref[...] = y * lg_ref[...] + lb_ref[...]


def _ffn_dense(x, mod3, w_in, w_out, ln_g, ln_b, m_prompt, t_sample):
    m = x.shape[0]
    tm, tf = FFN_TM, FFN_TF
    nf = D_FF // tf
    req = functools.partial(_request_of_tile, tm=tm, m_prompt=m_prompt, t_sample=t_sample)
    modspec = lambda k: pl.BlockSpec((1, 1, D_MODEL), lambda i, f: (req(i) * 6 + k, 0, 0))
    vec = pl.BlockSpec((1, D_MODEL), lambda i, f: (0, 0))
    return pl.pallas_call(
        _ffn_kernel,
        grid=(m // tm, nf),
        in_specs=[
            pl.BlockSpec((tm, D_MODEL), lambda i, f: (i, 0)),
            modspec(3), modspec(4), modspec(5),
            pl.BlockSpec((D_MODEL, tf), lambda i, f: (0, f)),
            pl.BlockSpec((D_MODEL, tf), lambda i, f: (0, nf + f)),
            pl.BlockSpec((tf, D_MODEL), lambda i, f: (f, 0)),
            vec, vec,
        ],
        out_specs=pl.BlockSpec((tm, D_MODEL), lambda i, f: (i, 0), pipeline_mode=pl.Buffered(1)),
        out_shape=jax.ShapeDtypeStruct((m, D_MODEL), F32),
        scratch_shapes=[pltpu.VMEM((tm, D_MODEL), BF16)],
        compiler_params=_params(("parallel", "arbitrary")),
        name="ffn_dense",
    )(x, mod3, mod3, mod3, w_in, w_in, w_out, ln_g, ln_b)


def _route_kernel(x_ref, sh_ref, sc_ref, wr_ref, h_ref, idx_ref, gate_ref):
    h = _layer_norm(x_ref[...]) * (1.0 + sc_ref[0]) + sh_ref[0]
    h_ref[...] = h
    logits = _dot(h, wr_ref[...], precision=HIGHEST)
    lane = lax.broadcasted_iota(jnp.int32, logits.shape, 1)
    logits = jnp.where(lane < N_EXPERTS, logits, -jnp.inf)
    lane_f = lane.astype(F32)
    m1 = jnp.max(logits, axis=-1, keepdims=True)
    i1 = jnp.min(jnp.where(logits == m1, lane_f, float(LANES)), axis=-1, keepdims=True)
    rest = jnp.where(lane_f == i1, -jnp.inf, logits)
    m2 = jnp.max(rest, axis=-1, keepdims=True)
    i2 = jnp.min(jnp.where(rest == m2, lane_f, float(LANES)), axis=-1, keepdims=True)
    e = jnp.exp(m2 - m1)
    g1 = 1.0 / (1.0 + e)
    idx_ref[...] = jnp.where(lane == 0, i1, jnp.where(lane == 1, i2, 0.0)).astype(jnp.int32)
    gate_ref[...] = jnp.where(lane == 0, g1, jnp.where(lane == 1, e * g1, 0.0))


def _route(x, mod3, router_p, m_prompt, t_sample):
    m = x.shape[0]
    tm = ROUTE_TM
    req = functools.partial(_request_of_tile, tm=tm, m_prompt=m_prompt, t_sample=t_sample)
    modspec = lambda k: pl.BlockSpec((1, 1, D_MODEL), lambda i: (req(i) * 6 + k, 0, 0))
    return pl.pallas_call(
        _route_kernel,
        grid=(m // tm,),
        in_specs=[pl.BlockSpec((tm, D_MODEL), lambda i: (i, 0)), modspec(3), modspec(4),
                  pl.BlockSpec((D_MODEL, LANES), lambda i: (0, 0))],
        out_specs=[pl.BlockSpec((tm, D_MODEL), lambda i: (i, 0)), pl.BlockSpec((tm, LANES), lambda i: (i, 0)),
                   pl.BlockSpec((tm, LANES), lambda i: (i, 0))],
        out_shape=[jax.ShapeDtypeStruct((m, D_MODEL), F32), jax.ShapeDtypeStruct((m, LANES), jnp.int32),
                   jax.ShapeDtypeStruct((m, LANES), F32)],
        compiler_params=_params(("parallel",)),
        name="route",
    )(x, mod3, mod3, router_p)


def _row_copy(src_hbm, row, dst, dst_row, sem):
    return pltpu.make_async_copy(src_hbm.at[pl.ds(row, 1)], dst.at[pl.ds(dst_row, 1)], sem)


def _moe_kernel(tile_e_ref, tile_valid_ref, row_tok_ref, h_hbm, wg_ref, wu_ref, wo_ref, o_ref, x_scr, sem):
    i = pl.program_id(0)
    f = pl.program_id(1)
    valid = tile_valid_ref[i]
    n_sub = lax.shift_right_logical(valid + (MOE_SUB - 1), MOE_SUB.bit_length() - 1)
    n_rows = n_sub * MOE_SUB

    @pl.when(f == 0)
    def _():
        def start(r8, carry):
            for j in range(8):
                r = 8 * r8 + j
                _row_copy(h_hbm, row_tok_ref[0, 0, r], o_ref, r, sem).start(priority=j % 2)
            return carry

        lax.fori_loop(0, n_rows // 8, start, 0)
        for sb in range(MOE_TM // MOE_SUB):
            @pl.when(sb < n_sub)
            def _():
                rows = o_ref.at[pl.ds(sb * MOE_SUB, MOE_SUB)]
                pltpu.make_async_copy(rows, rows, sem).wait()
        for sb in range(MOE_TM // MOE_SUB):
            @pl.when(sb < n_sub)
            def _():
                x_scr[sb * MOE_SUB:(sb + 1) * MOE_SUB, :] = o_ref[sb * MOE_SUB:(sb + 1) * MOE_SUB, :].astype(BF16)
        o_ref[...] = jnp.zeros(o_ref.shape, F32)

    for nb in range(1, MOE_TM // MOE_SUB + 1):
        @pl.when(n_sub == nb)
        def _():
            rows = nb * MOE_SUB
            x = x_scr[:rows, :]
            g = _dot(x, wg_ref[0].astype(BF16))
            u = _dot(x, wu_ref[0].astype(BF16))
            o_ref[:rows, :] += _dot((g * _sigmoid(g) * u).astype(BF16), wo_ref[0].astype(BF16))


def _moe_experts(h, row_tok, tile_e, tile_valid, w_in, w_out):
    n_tiles = tile_e.shape[0]
    nf = D_FF_EXPERT // MOE_TF
    fidx = lambda i, f, tv: jnp.where(tv[i] > 0, f, nf - 1)
    grid_spec = pltpu.PrefetchScalarGridSpec(
        num_scalar_prefetch=2,
        grid=(n_tiles, nf),
        in_specs=[
            pl.BlockSpec((1, 1, MOE_TM), lambda i, f, te, tv: (i, 0, 0), memory_space=pltpu.SMEM),
            pl.BlockSpec(memory_space=pl.ANY),
            pl.BlockSpec((1, D_MODEL, MOE_TF), lambda i, f, te, tv: (te[i], 0, fidx(i, f, tv))),
            pl.BlockSpec((1, D_MODEL, MOE_TF), lambda i, f, te, tv: (te[i], 0, nf + fidx(i, f, tv))),
            pl.BlockSpec((1, MOE_TF, D_MODEL), lambda i, f, te, tv: (te[i], fidx(i, f, tv), 0)),
        ],
        out_specs=pl.BlockSpec((MOE_TM, D_MODEL), lambda i, f, te, tv: (i, 0), pipeline_mode=pl.Buffered(1)),
        scratch_shapes=[pltpu.VMEM((MOE_TM, D_MODEL), BF16), pltpu.SemaphoreType.DMA(())],
    )
    return pl.pallas_call(
        _moe_kernel,
        grid_spec=grid_spec,
        out_shape=jax.ShapeDtypeStruct((n_tiles * MOE_TM, D_MODEL), F32),
        compiler_params=_params(("arbitrary", "arbitrary")),
        name="moe_experts",
    )(tile_e, tile_valid, row_tok.reshape(n_tiles, 1, MOE_TM), h, w_in, w_in, w_out)


def _combine_kernel(pos_ref, pos_next_ref, ys_hbm, x_ref, gate_ref, g2_ref, lg_ref, lb_ref, op_ref, os_ref,
                    buf, sems, *, n_prompt_tiles):
    tm = COMB_TM
    i = pl.program_id(0)
    slot = lax.rem(i, 2)

    def gather(p_ref, s):
        def start(t, carry):
            for k in range(TOP_K):
                _row_copy(ys_hbm, p_ref[0, 0, t * TOP_K + k], buf.at[s, k], t, sems.at[s]).start(priority=k)
            return carry
        lax.fori_loop(0, tm, start, 0, unroll=4)

    @pl.when(i == 0)
    def _():
        gather(pos_ref, 0)

    @pl.when(i + 1 < pl.num_programs(0))
    def _():
        gather(pos_next_ref, 1 - slot)

    pltpu.make_async_copy(buf.at[slot], buf.at[slot], sems.at[slot]).wait()
    y = gate_ref[:, 0:1] * buf[slot, 0] + gate_ref[:, 1:2] * buf[slot, 1]
    z = _layer_norm(ALPHA * x_ref[...] + g2_ref[0] * y)
    out = z * lg_ref[...] + lb_ref[...]

    @pl.when(i < n_prompt_tiles)
    def _():
        op_ref[...] = out

    @pl.when(i >= n_prompt_tiles)
    def _():
        os_ref[...] = out


def _moe_combine(x, ys, pos, gates, mod3, ln_g, ln_b, m_prompt, t_sample):
    m = x.shape[0]
    tm = COMB_TM
    n = m // tm
    n_p = m_prompt // tm
    req = functools.partial(_request_of_tile, tm=tm, m_prompt=m_prompt, t_sample=t_sample)
    pos3 = pos.reshape(n, 1, tm * TOP_K)
    return pl.pallas_call(
        functools.partial(_combine_kernel, n_prompt_tiles=n_p),
        grid=(n,),
        in_specs=[
            pl.BlockSpec((1, 1, tm * TOP_K), lambda i: (i, 0, 0), memory_space=pltpu.SMEM),
            pl.BlockSpec((1, 1, tm * TOP_K), lambda i: (jnp.minimum(i + 1, n - 1), 0, 0), memory_space=pltpu.SMEM),
            pl.BlockSpec(memory_space=pl.ANY),
            pl.BlockSpec((tm, D_MODEL), lambda i: (i, 0)),
            pl.BlockSpec((tm, LANES), lambda i: (i, 0)),
            pl.BlockSpec((1, 1, D_MODEL), lambda i: (req(i) * 6 + 5, 0, 0)),
            pl.BlockSpec((1, D_MODEL), lambda i: (0, 0)),
            pl.BlockSpec((1, D_MODEL), lambda i: (0, 0)),
        ],
        out_specs=[pl.BlockSpec((tm, D_MODEL), lambda i: (jnp.minimum(i, n_p - 1), 0)),
                   pl.BlockSpec((tm, D_MODEL), lambda i: (jnp.maximum(i - n_p, 0), 0))],
        out_shape=[jax.ShapeDtypeStruct((m_prompt, D_MODEL), F32),
                   jax.ShapeDtypeStruct((m - m_prompt, D_MODEL), F32)],
        scratch_shapes=[pltpu.VMEM((2, TOP_K, tm, D_MODEL), F32), pltpu.SemaphoreType.DMA((2,))],
        compiler_params=_params(("arbitrary",)),
        name="moe_combine",
    )(pos3, pos3, ys, x, gates, mod3, ln_g, ln_b)


def _moe_plan(idx):
    m = idx.shape[0]
    flat_e = idx.reshape(-1)
    n_assign = m * TOP_K
    onehot = (flat_e[:, None] == jnp.arange(N_EXPERTS, dtype=jnp.int32)[None, :]).astype(jnp.int32)
    csum = jnp.cumsum(onehot, axis=0)
    counts = csum[-1]
    rank = jnp.take_along_axis(csum, flat_e[:, None], axis=1)[:, 0] - 1
    padded = (counts + MOE_TM - 1) // MOE_TM * MOE_TM
    pad_end = jnp.cumsum(padded)
    pad_start = pad_end - padded
    pos = (pad_start[flat_e] + rank).astype(jnp.int32)
    n_tiles = n_assign // MOE_TM + N_EXPERTS
    row_tok = jnp.zeros((n_tiles * MOE_TM,), jnp.int32).at[pos].set(jnp.arange(n_assign, dtype=jnp.int32) // TOP_K)
    tile_start = jnp.arange(n_tiles, dtype=jnp.int32) * MOE_TM
    tile_e = jnp.sum((pad_end[None, :] <= tile_start[:, None]).astype(jnp.int32), axis=1)
    used = tile_e < N_EXPERTS
    last_e = jnp.max(jnp.where(counts > 0, jnp.arange(N_EXPERTS, dtype=jnp.int32), 0))
    tile_e = jnp.where(used, tile_e, last_e)
    seg_end = pad_start[tile_e] + counts[tile_e]
    tile_valid = jnp.where(used, jnp.clip(seg_end - tile_start, 0, MOE_TM), 0).astype(jnp.int32)
    return row_tok, tile_e, tile_valid, pos


def _reorder_w_in(w_in):
    o_q, o_k, o_v = 0, A_WIDTH, A_WIDTH + A_KV_WIDTH
    o_xb = o_v + A_KV_WIDTH
    o_xc = o_xb + 2 * SGU_WIDTH
    o_lora = o_xc + 3 * C_WIDTH
    o_gl = o_xc + C_COLS
    cols = lambda a, n: w_in[:, :, a:a + n]
    pad = jnp.zeros(w_in.shape[:2] + (COL_TILE - 2 * A_KV_WIDTH - LORA_COLS,), w_in.dtype)
    parts = [cols(o_gl, N_BRANCH * D_MODEL), cols(o_xb, 2 * SGU_WIDTH), cols(o_q, A_WIDTH),
             cols(o_xc, 3 * C_WIDTH), cols(o_k, 2 * A_KV_WIDTH), cols(o_lora, LORA_COLS), pad]
    return jnp.concatenate(parts, axis=-1).astype(BF16)


def kernel(x_prompt, x_sample, cache_k, cache_v, state_rwkv, c, c_ctx, w_mod, b_mod, w_in, b_gate, attn_sink, sgu_norm_g, sgu_w, sgu_b, rwkv_mu, rwkv_w0, rwkv_w2, rwkv_a0, rwkv_a2, rwkv_g2, rwkv_k_k, rwkv_k_a, rwkv_r_k, rwkv_ln_g, rwkv_ln_b, w_branch_a, w_branch_b, w_branch_c, w_out, ln1_g, ln1_b, ln2_g, ln2_b, ffn_w_in, ffn_w_out, moe_router, moe_w_in, moe_w_out):
    bp, tp, _ = x_prompt.shape
    bs, ts, _ = x_sample.shape
    past = cache_k.shape[2]
    mp, ms = bp * tp, bs * ts
    assert bs + 1 <= 16 and mp % IN_TM == 0 and ts % IN_TM == 0 and tp % PRE_TM == 0 and IN_TM % tp == 0

    x = jnp.concatenate([x_prompt.reshape(mp, D_MODEL), x_sample.reshape(ms, D_MODEL)], axis=0)
    cond = jnp.zeros((16, D_MODEL), F32).at[0].set(c_ctx).at[1:1 + bs].set(c)
    mod = _modulation(cond, w_mod, b_mod)

    w_in_p = _reorder_w_in(w_in)
    cos, sin = _rope_tables(ts)
    ones_bd = jnp.kron(jnp.eye(LANES // HEAD_DIM, dtype=F32), jnp.ones((HEAD_DIM, HEAD_DIM), F32)).astype(BF16)
    cache_k4 = cache_k.reshape(bs, DEPTH, past, A_KV_WIDTH)
    cache_v4 = cache_v.reshape(bs, DEPTH, past, A_KV_WIDTH)
    states4 = state_rwkv.astype(F32).reshape(bs, DEPTH * 2, C_HEADS, HEAD_DIM, HEAD_DIM)

    new_k, new_v, new_s = [], [], []
    for l in range(DEPTH):
        mod3 = mod[l].reshape(16 * 6, 1, D_MODEL)
        lp = {
            "mu_rkv": rwkv_mu[l][:, :3 * C_WIDTH], "mu_lora": rwkv_mu[l][:, 3 * C_WIDTH:],
            "k_k": rwkv_k_k[l].reshape(1, C_WIDTH), "k_a": rwkv_k_a[l].reshape(1, C_WIDTH),
            "r_k": rwkv_r_k[l].reshape(1, C_WIDTH), "w0": rwkv_w0[l], "a0": rwkv_a0[l],
            "w2": rwkv_w2[l].astype(BF16), "a2": rwkv_a2[l].astype(BF16), "g2": rwkv_g2[l].astype(BF16),
            "b_gate": b_gate[l].reshape(1, N_BRANCH * D_MODEL),
            "w_branch_a": w_branch_a[l].astype(BF16), "w_branch_b": w_branch_b[l].astype(BF16),
            "w_branch_c": w_branch_c[l].astype(BF16), "w_out": w_out[l].astype(BF16),
            "ln1_g": ln1_g[l].reshape(1, D_MODEL), "ln1_b": ln1_b[l].reshape(1, D_MODEL),
        }
        proj = _in_proj(x, mod3, w_in_p[l], mp, ts)

        attn_p = _attn_context(proj, attn_sink[l], bp, tp)
        attn_s = _attn_latent(proj, attn_sink[l], cache_k4, cache_v4, l, cos, sin, mp, bs, ts)
        kv_cols = proj[:mp, CT_MISC * COL_TILE:CT_MISC * COL_TILE + 2 * A_KV_WIDTH]
        new_k.append(kv_cols[:, :A_KV_WIDTH].reshape(bp, tp, A_KV_HEADS, HEAD_DIM))
        new_v.append(kv_cols[:, A_KV_WIDTH:].reshape(bp, tp, A_KV_HEADS, HEAD_DIM))

        sgu_bias = jnp.broadcast_to(sgu_b[l][:, :, None], (SGU_GROUPS, SGU_CHUNK, LANES))
        sgu = _sgu(proj, sgu_norm_g[l].reshape(1, SGU_WIDTH), sgu_w[l].astype(BF16), sgu_bias)

        pre = _rwkv_pre(proj, lp, ones_bd, mp, ts)
        yf_p, yb_p, s_fin = _rwkv_scan(pre, None, l, 0, bp, tp)
        yf_s, yb_s = _rwkv_scan(pre, states4, l, mp, bs, ts)
        rw = _rwkv_post((yf_p, yb_p), (yf_s, yb_s), pre[4], pre[3],
                        rwkv_ln_g[l].reshape(1, C_WIDTH), rwkv_ln_b[l].reshape(1, C_WIDTH), ones_bd)
        new_s.append(s_fin)

        x1 = _merge(x, attn_p, attn_s, sgu, rw, proj, mod3, lp, mp, ts)

        ln2g, ln2b = ln2_g[l].reshape(1, D_MODEL), ln2_b[l].reshape(1, D_MODEL)
        if l % 2 == 0:
            x = _ffn_dense(x1, mod3, ffn_w_in[l // 2].astype(BF16), ffn_w_out[l // 2].astype(BF16), ln2g, ln2b, mp, ts)
        else:
            router_p = jnp.zeros((D_MODEL, LANES), F32).at[:, :N_EXPERTS].set(moe_router[l // 2])
            h, idx, gates = _route(x1, mod3, router_p, mp, ts)
            row_tok, tile_e, tile_valid, pos = _moe_plan(idx[:, :TOP_K])
            ys = _moe_experts(h, row_tok, tile_e, tile_valid, moe_w_in[l // 2], moe_w_out[l // 2])
            groups = _moe_combine(x1, ys, pos, gates, mod3, ln2g, ln2b, mp, ts)
            if l + 1 < DEPTH:
                x = jnp.concatenate(groups, axis=0)
    if DEPTH % 2 == 1:
        groups = (x[:mp], x[mp:])

    y_prompt = groups[0].reshape(bp, tp, D_MODEL)
    y_sample = groups[1].reshape(bs, ts, D_MODEL)
    return (y_prompt, y_sample, jnp.stack(new_k, axis=1), jnp.stack(new_v, axis=1), jnp.stack(new_s, axis=1))
```
